```python
import math
import numpy as np
import jax
import jax.numpy as jnp
from jax import lax

D_MODEL = 1024
BATCH = 16
SEQ = 2048
DEPTH = 2

EXPAND = 2
MIX_WIDTH = EXPAND * D_MODEL
HG_WIDTH = MIX_WIDTH // 2
HG_HEAD_DIM = 128
HG_HEADS = HG_WIDTH // HG_HEAD_DIM
HG_CHUNK = 64
ATT_WIDTH = MIX_WIDTH - HG_WIDTH
ATT_HEAD_DIM = 64
ATT_HEADS = ATT_WIDTH // ATT_HEAD_DIM
ATT_KV_HEADS = max(1, ATT_HEADS // 8)
ATT_GROUP = ATT_HEADS // ATT_KV_HEADS
KV_WIDTH = ATT_KV_HEADS * ATT_HEAD_DIM
WINDOW = 128
ATT_BLOCK = 128
ATT_SCALE = 1.0 / math.sqrt(ATT_HEAD_DIM)
ROPE_THETA = 10000.0
NORM_EPS = 1e-6
NEG_INF = -1e30
LB_FLOOR = 1e-20

SPLIT_SIZES = (HG_WIDTH, HG_WIDTH, HG_WIDTH, HG_WIDTH, ATT_WIDTH, KV_WIDTH, KV_WIDTH, ATT_WIDTH)
IN_WIDTH = int(sum(SPLIT_SIZES))
SPLIT_POINTS = tuple(int(v) for v in np.cumsum(SPLIT_SIZES)[:-1])

kernel_name = "hymba_hgrn2_swa_sink_hybrid"


def rms_norm(x, g):
    xf = x.astype(jnp.float32)
    y = xf * lax.rsqrt(jnp.mean(xf * xf, axis=-1, keepdims=True) + NORM_EPS)
    return (y * g.astype(jnp.float32)).astype(x.dtype)


def rotary(x, pos):
    half = x.shape[-1] // 2
    inv_freq = ROPE_THETA ** (-jnp.arange(half, dtype=jnp.float32) / half)
    ang = pos.astype(jnp.float32)[:, None] * inv_freq[None, :]
    cos = jnp.cos(ang)[None, :, None, :]
    sin = jnp.sin(ang)[None, :, None, :]
    xf = x.astype(jnp.float32)
    x1, x2 = xf[..., :half], xf[..., half:]
    return jnp.concatenate([x1 * cos - x2 * sin, x2 * cos + x1 * sin], axis=-1).astype(x.dtype)


def hgrn2(q, f_logit, i, lb):
    B, S, _ = q.shape
    C, H, D = HG_CHUNK, HG_HEADS, HG_HEAD_DIM
    nC = S // C
    lb = lb.astype(jnp.float32)
    qf = jax.nn.silu(q.astype(jnp.float32))
    logf = jnp.logaddexp(jnp.log(jnp.maximum(lb, LB_FLOOR)),
                         jnp.log1p(-lb) + jax.nn.log_sigmoid(f_logit.astype(jnp.float32)))
    k = -jnp.expm1(logf)

    def chunks(t):
        return t.reshape(B, nC, C, H, D).transpose(1, 0, 3, 2, 4)

    qc, kc, vc, gc = chunks(qf), chunks(k), chunks(i.astype(jnp.float32)), chunks(logf)
    bc = jnp.cumsum(gc, axis=3)
    causal = jnp.tril(jnp.ones((C, C), dtype=bool))[None, None, :, :, None]

    def step(state, xs):
        qch, kch, vch, b = xs
        b_last = b[:, :, C - 1:C, :]
        diff = b[:, :, :, None, :] - b[:, :, None, :, :]
        decay = jnp.where(causal, jnp.exp(jnp.where(causal, diff, 0.0)), 0.0)
        scores = jnp.einsum('bhtd,bhsd,bhtsd->bhts', qch, kch, decay)
        o = jnp.einsum('bhts,bhsv->bhtv', scores, vch)
        o = o + jnp.einsum('bhtd,bhdv->bhtv', qch * jnp.exp(b), state)
        state = jnp.exp(b_last[:, :, 0, :])[..., None] * state + \
            jnp.einsum('bhsd,bhsv->bhdv', kch * jnp.exp(b_last - b), vch)
        return state, o

    s0 = jnp.zeros((B, H, D, D), dtype=jnp.float32)
    _, o = lax.scan(step, s0, (qc, kc, vc, bc))
    return o.transpose(1, 0, 3, 2, 4).reshape(B, S, H, D)


def sliding_window_attention(q, k, v, sinks):
    B, S = q.shape[0], q.shape[1]
    L, KV, G, D = ATT_BLOCK, ATT_KV_HEADS, ATT_GROUP, ATT_HEAD_DIM
    nB = S // L
    qb = q.reshape(B, nB, L, KV, G, D)
    kb = k.reshape(B, nB, L, KV, D)
    vb = v.reshape(B, nB, L, KV, D)
    k_prev = jnp.concatenate([jnp.zeros_like(kb[:, :1]), kb[:, :-1]], axis=1)
    v_prev = jnp.concatenate([jnp.zeros_like(vb[:, :1]), vb[:, :-1]], axis=1)
    kw = jnp.concatenate([k_prev, kb], axis=2)
    vw = jnp.concatenate([v_prev, vb], axis=2)
    s = jnp.einsum('bnqkgd,bnskd->bnkgqs', qb, kw).astype(jnp.float32) * ATT_SCALE
    qpos = jnp.arange(L)[:, None] + L
    kpos = jnp.arange(2 * L)[None, :]
    diff = qpos - kpos
    band = (diff >= 0) & (diff < WINDOW)
    key_exists = (jnp.arange(nB)[:, None] * L - L + jnp.arange(2 * L)[None, :]) >= 0
    mask = band[None, :, :] & key_exists[:, None, :]
    s = jnp.where(mask[None, :, None, None, :, :], s, NEG_INF)
    sink = jnp.broadcast_to(sinks.astype(jnp.float32).reshape(KV, G)[None, None, :, :, None, None],
                            s.shape[:-1] + (1,))
    p = jax.nn.softmax(jnp.concatenate([s, sink], axis=-1), axis=-1)[..., :-1]
    o = jnp.einsum('bnkgqs,bnskd->bnqkgd', p.astype(v.dtype), vw)
    return o.reshape(B, S, KV * G * D)


def hybrid_layer(x, w_in, w_out, g_pre, g_post, lb, g_head, sinks):
    B, S, _ = x.shape
    h = rms_norm(x, g_pre)
    proj = jnp.einsum('bsd,de->bse', h, w_in)
    q_h, f_h, i_h, z_h, q_a, k_a, v_a, z_a = jnp.split(proj, SPLIT_POINTS, axis=-1)

    o_h = hgrn2(q_h, f_h, i_h, lb)
    o_h = rms_norm(o_h, g_head).reshape(B, S, HG_WIDTH).astype(z_h.dtype) * jax.nn.silu(z_h)

    pos = jnp.arange(S)
    q_a = rotary(q_a.reshape(B, S, ATT_HEADS, ATT_HEAD_DIM), pos)
    k_a = rotary(k_a.reshape(B, S, ATT_KV_HEADS, ATT_HEAD_DIM), pos)
    v_a = v_a.reshape(B, S, ATT_KV_HEADS, ATT_HEAD_DIM)
    o_a = sliding_window_attention(q_a, k_a, v_a, sinks).astype(z_a.dtype) * jax.nn.silu(z_a)

    y = jnp.einsum('bse,ed->bsd', jnp.concatenate([o_h, o_a], axis=-1), w_out)
    return x + rms_norm(y, g_post)


def setup_inputs(seed: int = 0) -> dict:
    key = jax.random.key(seed)
    ks = jax.random.split(key, 9)
    x = jax.random.normal(ks[0], (BATCH, SEQ, D_MODEL), dtype=jnp.float32)
    w_in = jax.random.normal(ks[1], (DEPTH, D_MODEL, IN_WIDTH), dtype=jnp.float32) * D_MODEL ** -0.5
    w_out = jax.random.normal(ks[2], (DEPTH, MIX_WIDTH, D_MODEL), dtype=jnp.float32) * MIX_WIDTH ** -0.5
    g_pre = 1.0 + 0.05 * jax.random.normal(ks[3], (DEPTH, D_MODEL), dtype=jnp.float32)
    g_post = 1.0 + 0.05 * jax.random.normal(ks[4], (DEPTH, D_MODEL), dtype=jnp.float32)
    lb_param = 0.1 * jax.random.normal(ks[5], (DEPTH, HG_WIDTH), dtype=jnp.float32)
    g_head = 1.0 + 0.05 * jax.random.normal(ks[6], (DEPTH, HG_HEAD_DIM), dtype=jnp.float32)
    sinks = jax.random.normal(ks[7], (DEPTH, ATT_HEADS), dtype=jnp.float32)
    return {"x": x, "w_in": w_in, "w_out": w_out, "g_pre": g_pre, "g_post": g_post,
            "lb_param": lb_param, "g_head": g_head, "sinks": sinks}


def reference(x, w_in, w_out, g_pre, g_post, lb_param, g_head, sinks):
    p = jax.nn.softmax(lb_param.astype(jnp.float32), axis=0)
    lower_bounds = jnp.cumsum(p, axis=0) - p[0:1]
    for l in range(DEPTH):
        x = hybrid_layer(x, w_in[l], w_out[l], g_pre[l], g_post[l], lower_bounds[l], g_head[l], sinks[l])
    return x
```

```python
import functools
import math

import numpy as np
import jax
import jax.numpy as jnp
from jax import lax
from jax.experimental import pallas as pl
from jax.experimental.pallas import tpu as pltpu

D_MODEL = 1024
HG_WIDTH = 1024
HG_HEADS = 8
HG_DIM = 128
ATT_WIDTH = 1024
ATT_HEADS = 16
ATT_DIM = 64
ATT_PAIRS = ATT_HEADS // 2
KV_HEADS = 2
KV_WIDTH = 128
ATT_GROUP = ATT_HEADS // KV_HEADS
WINDOW = 128
IN_WIDTH = 6400
MIX_WIDTH = 2048
ATT_SCALE = 1.0 / math.sqrt(ATT_DIM)
ROPE_THETA = 10000.0
NORM_EPS = 1e-6
NEG_INF = -1e30
LB_FLOOR = 1e-20

LANES = 128
CHUNK = 128
LEVELS = 7
SEQ_TILE = 256
COL_BLOCK = 256
ROW_BLOCK = 64
VMEM_LIMIT_BYTES = 56 * 1024 * 1024

OFF_QH, OFF_F, OFF_I, OFF_ZH, OFF_QA, OFF_KV, OFF_ZA = 0, 1024, 2048, 3072, 4096, 5120, 5376

F32 = jnp.float32
BF16 = jnp.bfloat16


def _dot(a, b):
    return jnp.dot(a, b, preferred_element_type=F32)


def _dot_nt(a, b):
    return lax.dot_general(a, b, (((1,), (1,)), ((), ())), preferred_element_type=F32)


def _dot_tn(a, b):
    return lax.dot_general(a, b, (((0,), (0,)), ((), ())), preferred_element_type=F32)


def _silu(x):
    return x * jax.nn.sigmoid(x)


def _rope(x, cos, sin, lo_half):
    swapped = jnp.where(lo_half, pltpu.roll(x, LANES - ATT_DIM // 2, 1), pltpu.roll(x, ATT_DIM // 2, 1))
    return x * cos + swapped * sin


def _hold_rows(b, period, offset):
    pieces = []
    for blk in range(CHUNK // period):
        row = blk * period + offset
        pieces.append(jnp.broadcast_to(b[row:row + 1, :], (period, LANES)))
    return pieces[0] if len(pieces) == 1 else jnp.concatenate(pieces, axis=0)


def _separator_distance(b, lf, level, row_idx):
    half = 1 << level
    if half >= 4:
        r = _hold_rows(b, 2 * half, half - 1)
        return jnp.abs(b - r)
    if half == 2:
        r = jnp.where((row_idx & 7) < 4, _hold_rows(b, 8, 1), _hold_rows(b, 8, 5))
        return jnp.abs(b - r)
    return jnp.where((row_idx & 1) == 1, -lf, 0.0)


def _layer_kernel(x_ref, win_ref, wout_ref, gpre_ref, gpost_ref, lbf_ref, oml_ref, lbd_ref, ghead_ref,
                  sink_ref, cosq_ref, sinq_ref, cosk_ref, sink_rot_ref, lvl_ref,
                  o_ref,
                  hn_s, qh_s, kh_s, vh_s, lf_s, b_s, zh_s, qa_s, kpad_s, vpad_s, za_s, yin_s, state_s):
    T = SEQ_TILE
    j = pl.program_id(1)

    @pl.when(j == 0)
    def _():
        state_s[...] = jnp.zeros_like(state_s)
        kpad_s[:, 0:CHUNK, :] = jnp.zeros((2 * KV_HEADS, CHUNK, LANES), BF16)
        vpad_s[:, 0:CHUNK, :] = jnp.zeros((2 * KV_HEADS, CHUNK, LANES), BF16)

    for rb in range(T // ROW_BLOCK):
        rows = slice(rb * ROW_BLOCK, (rb + 1) * ROW_BLOCK)
        x = x_ref[0, rows, :]
        ms = jnp.mean(x * x, axis=-1, keepdims=True)
        hn_s[rows, :] = (x * lax.rsqrt(ms + NORM_EPS) * gpre_ref[...]).astype(BF16)

    lane = lax.broadcasted_iota(jnp.int32, (T, LANES), 1)
    lo_half = (lane & (ATT_DIM - 1)) < ATT_DIM // 2
    lo_head = lane < ATT_DIM

    def proj(col):
        return _dot(hn_s[...], win_ref[:, col:col + COL_BLOCK])

    for c in range(HG_WIDTH // COL_BLOCK):
        res = proj(OFF_QH + c * COL_BLOCK)
        qf = _silu(res)
        qh_s[2 * c] = qf[:, :LANES]
        qh_s[2 * c + 1] = qf[:, LANES:]

    for c in range(HG_WIDTH // COL_BLOCK):
        cols = slice(c * COL_BLOCK, (c + 1) * COL_BLOCK)
        res = proj(OFF_F + c * COL_BLOCK)
        sig = jax.nn.sigmoid(res)
        oml = oml_ref[:, cols]
        f = lbf_ref[:, cols] + oml * sig
        lf = jnp.log(f)
        k = oml * (1.0 - sig) - lbd_ref[:, cols]
        lf_s[2 * c] = lf[:, :LANES]
        lf_s[2 * c + 1] = lf[:, LANES:]
        kh_s[2 * c] = k[:, :LANES]
        kh_s[2 * c + 1] = k[:, LANES:]

    for c in range(HG_WIDTH // COL_BLOCK):
        res = proj(OFF_I + c * COL_BLOCK).astype(BF16)
        vh_s[2 * c] = res[:, :LANES]
        vh_s[2 * c + 1] = res[:, LANES:]

    for c in range(HG_WIDTH // COL_BLOCK):
        res = proj(OFF_ZH + c * COL_BLOCK)
        zh_s[2 * c] = res[:, :LANES]
        zh_s[2 * c + 1] = res[:, LANES:]

    cosq = cosq_ref[...]
    sinq = sinq_ref[...]
    for c in range(ATT_WIDTH // COL_BLOCK):
        res = proj(OFF_QA + c * COL_BLOCK)
        qa_s[2 * c] = _rope(res[:, :LANES], cosq, sinq, lo_half).astype(BF16)
        qa_s[2 * c + 1] = _rope(res[:, LANES:], cosq, sinq, lo_half).astype(BF16)

    res = proj(OFF_KV)
    kr = _rope(res[:, :LANES], cosk_ref[...], sink_rot_ref[...], lo_half)
    kr_sw = pltpu.roll(kr, ATT_DIM, 1)
    va = res[:, LANES:]
    va_sw = pltpu.roll(va, ATT_DIM, 1)
    cur = slice(CHUNK, CHUNK + T)
    zero = jnp.zeros((T, LANES), F32)
    kpad_s[0, cur, :] = jnp.where(lo_head, kr, zero).astype(BF16)
    kpad_s[1, cur, :] = jnp.where(lo_head, zero, kr_sw).astype(BF16)
    kpad_s[2, cur, :] = jnp.where(lo_head, kr_sw, zero).astype(BF16)
    kpad_s[3, cur, :] = jnp.where(lo_head, zero, kr).astype(BF16)
    vpad_s[0, cur, :] = jnp.where(lo_head, va, zero).astype(BF16)
    vpad_s[1, cur, :] = jnp.where(lo_head, zero, va_sw).astype(BF16)
    vpad_s[2, cur, :] = jnp.where(lo_head, va_sw, zero).astype(BF16)
    vpad_s[3, cur, :] = jnp.where(lo_head, zero, va).astype(BF16)

    for c in range(ATT_WIDTH // COL_BLOCK):
        res = proj(OFF_ZA + c * COL_BLOCK)
        za_s[2 * c] = res[:, :LANES]
        za_s[2 * c + 1] = res[:, LANES:]

    row_c = lax.broadcasted_iota(jnp.int32, (CHUNK, LANES), 0)
    col_c = lax.broadcasted_iota(jnp.int32, (CHUNK, LANES), 1)
    tri = (row_c >= col_c).astype(BF16)
    causal = row_c >= col_c
    for ch in range(T // CHUNK):
        rows = slice(ch * CHUNK, (ch + 1) * CHUNK)
        lf = jnp.concatenate([lf_s[h, rows, :] for h in range(HG_HEADS)], axis=1)
        hi = lf.astype(BF16)
        lo = (lf - hi.astype(F32)).astype(BF16)
        b = _dot(tri, hi) + _dot(tri, lo)
        for h in range(HG_HEADS):
            b_s[h, rows, :] = b[:, h * LANES:(h + 1) * LANES]

    lvl = lvl_ref[...]
    ghead = ghead_ref[...]

    def hgrn_head(h, ch):
        rows = pl.ds(ch * CHUNK, CHUNK)
        q = qh_s[h, rows, :]
        k = kh_s[h, rows, :]
        v = vh_s[h, rows, :]
        b = b_s[h, rows, :]
        lf = lf_s[h, rows, :]
        st = state_s[h]

        scores = jnp.zeros((CHUNK, CHUNK), F32)
        for level in range(LEVELS):
            half = 1 << level
            dist = _separator_distance(b, lf, level, row_c)
            e = jnp.exp(-dist)
            is_right = (row_c & (2 * half - 1)) >= half
            xk = (jnp.where(is_right, q, k) * e).astype(BF16)
            scores = jnp.where(lvl == level, _dot_nt(xk, xk), scores)

        b_last = b[CHUNK - 1:CHUNK, :]
        q_in = (q * jnp.exp(b)).astype(BF16)
        o = _dot(scores.astype(BF16), v) + _dot_nt(q_in, st.astype(BF16))
        o = o + jnp.sum(q * k, axis=-1, keepdims=True) * v.astype(F32)

        k_out = (k * jnp.exp(b_last - b)).astype(BF16)
        state_s[h] = st * jnp.exp(b_last) + _dot_tn(v, k_out)

        ms = jnp.mean(o * o, axis=-1, keepdims=True)
        on = o * lax.rsqrt(ms + NORM_EPS) * ghead
        yin_s[h, rows, :] = (on * _silu(zh_s[h, rows, :])).astype(BF16)

    for ch in range(T // CHUNK):
        def head_body(h, carry, ch=ch):
            hgrn_head(h, ch)
            return carry
        lax.fori_loop(0, HG_HEADS, head_body, 0)

    def attn_pair(p, n):
        g = p // (ATT_PAIRS // KV_HEADS)
        qrows = pl.ds(n * CHUNK, CHUNK)
        krows = pl.ds(n * CHUNK, 2 * CHUNK)
        qp = qa_s[p, qrows, :]
        prev_ok = jnp.logical_or(j > 0, n > 0)
        out = jnp.zeros((CHUNK, LANES), F32)
        for pos in range(2):
            kp = kpad_s[2 * g + pos, krows, :]
            vp = vpad_s[2 * g + pos, krows, :]
            s = _dot_nt(qp, kp)
            s_prev = jnp.where(prev_ok, s[:, :CHUNK], NEG_INF)
            sc = jnp.where(causal, s[:, CHUNK:], s_prev)
            sink = sink_ref[2 * p + pos]
            m = jnp.maximum(jnp.max(sc, axis=-1, keepdims=True), sink)
            pr = jnp.exp(sc - m)
            denom = jnp.sum(pr, axis=-1, keepdims=True) + jnp.exp(sink - m)
            pr = pr * (1.0 / denom)
            pcat = jnp.concatenate([jnp.where(causal, 0.0, pr), jnp.where(causal, pr, 0.0)], axis=1)
            out = out + _dot(pcat.astype(BF16), vp)
        yin_s[HG_HEADS + p, qrows, :] = (out * _silu(za_s[p, qrows, :])).astype(BF16)

    for n in range(T // CHUNK):
        def pair_body(p, carry, n=n):
            attn_pair(p, n)
            return carry
        lax.fori_loop(0, ATT_PAIRS, pair_body, 0)

    kpad_s[:, 0:CHUNK, :] = kpad_s[:, T:T + CHUNK, :]
    vpad_s[:, 0:CHUNK, :] = vpad_s[:, T:T + CHUNK, :]

    yin = jnp.concatenate([yin_s[i] for i in range(MIX_WIDTH // LANES)], axis=1)
    y = _dot(yin, wout_ref[...])
    ms = jnp.mean(y * y, axis=-1, keepdims=True)
    o_ref[0] = x_ref[0] + y * lax.rsqrt(ms + NORM_EPS) * gpost_ref[...]


def _level_matrix():
    t = np.arange(CHUNK)[:, None]
    s = np.arange(CHUNK)[None, :]
    x = np.bitwise_xor(t, s)
    msb = np.where(x > 0, np.floor(np.log2(np.maximum(x, 1))), -1).astype(np.int32)
    return np.where(s < t, msb, -1).astype(np.int32)


def _rope_tables(seq):
    half = ATT_DIM // 2
    inv_freq = ROPE_THETA ** (-jnp.arange(half, dtype=F32) / half)
    ang = jnp.arange(seq, dtype=F32)[:, None] * inv_freq[None, :]
    cos = jnp.tile(jnp.cos(ang), (1, LANES // half))
    sin = jnp.sin(ang)
    sin = jnp.tile(jnp.concatenate([-sin, sin], axis=1), (1, LANES // ATT_DIM))
    return cos, sin


def _layer(x, w_in, w_out, g_pre, g_post, lb, g_head, sinks, tables):
    B, S, _ = x.shape
    T = SEQ_TILE
    cosq, sinq, cosk, sink_rot, lvl = tables
    lbf = jnp.maximum(lb, LB_FLOOR)

    def const(shape):
        return pl.BlockSpec(shape, lambda b, j: (0,) * len(shape))

    def seq_table():
        return pl.BlockSpec((T, LANES), lambda b, j: (j, 0))

    in_specs = [
        pl.BlockSpec((1, T, D_MODEL), lambda b, j: (b, j, 0)),
        const((D_MODEL, IN_WIDTH)),
        const((MIX_WIDTH, D_MODEL)),
        const((1, D_MODEL)),
        const((1, D_MODEL)),
        const((1, HG_WIDTH)),
        const((1, HG_WIDTH)),
        const((1, HG_WIDTH)),
        const((1, HG_DIM)),
        pl.BlockSpec(memory_space=pltpu.SMEM),
        seq_table(), seq_table(), seq_table(), seq_table(),
        const((CHUNK, CHUNK)),
    ]
    scratch = [
        pltpu.VMEM((T, D_MODEL), BF16),
        pltpu.VMEM((HG_HEADS, T, LANES), F32),
        pltpu.VMEM((HG_HEADS, T, LANES), F32),
        pltpu.VMEM((HG_HEADS, T, LANES), BF16),
        pltpu.VMEM((HG_HEADS, T, LANES), F32),
        pltpu.VMEM((HG_HEADS, T, LANES), F32),
        pltpu.VMEM((HG_HEADS, T, LANES), F32),
        pltpu.VMEM((ATT_PAIRS, T, LANES), BF16),
        pltpu.VMEM((2 * KV_HEADS, CHUNK + T, LANES), BF16),
        pltpu.VMEM((2 * KV_HEADS, CHUNK + T, LANES), BF16),
        pltpu.VMEM((ATT_PAIRS, T, LANES), F32),
        pltpu.VMEM((MIX_WIDTH // LANES, T, LANES), BF16),
        pltpu.VMEM((HG_HEADS, HG_DIM, HG_DIM), F32),
    ]
    return pl.pallas_call(
        _layer_kernel,
        out_shape=jax.ShapeDtypeStruct(x.shape, x.dtype),
        grid=(B, S // T),
        in_specs=in_specs,
        out_specs=pl.BlockSpec((1, T, D_MODEL), lambda b, j: (b, j, 0)),
        scratch_shapes=scratch,
        compiler_params=pltpu.CompilerParams(
            dimension_semantics=("arbitrary", "arbitrary"),
            vmem_limit_bytes=VMEM_LIMIT_BYTES),
        name="hybrid_layer",
    )(x, w_in.astype(BF16), w_out.astype(BF16), g_pre[None, :], g_post[None, :],
      lbf[None, :], (1.0 - lb)[None, :], (lbf - lb)[None, :], g_head[None, :], sinks,
      cosq, sinq, cosk, sink_rot, lvl)


def kernel(x, w_in, w_out, g_pre, g_post, lb_param, g_head, sinks):
    depth = w_in.shape[0]
    seq = x.shape[1]
    p = jax.nn.softmax(lb_param.astype(F32), axis=0)
    lower_bounds = jnp.cumsum(p, axis=0) - p[0:1]
    cos, sin = _rope_tables(seq)
    tables = (cos * ATT_SCALE, sin * ATT_SCALE, cos, sin, jnp.asarray(_level_matrix()))
    for l in range(depth):
        x = _layer(x, w_in[l], w_out[l], g_pre[l], g_post[l], lower_bounds[l], g_head[l], sinks[l], tables)
    return x
```

```python
import math

import numpy as np
import jax
import jax.numpy as jnp
from jax import lax
from jax.experimental import pallas as pl
from jax.experimental.pallas import tpu as pltpu

D_MODEL = 1024
HG_WIDTH = 1024
HG_HEADS = 8
HG_DIM = 128
ATT_WIDTH = 1024
ATT_HEADS = 16
ATT_DIM = 64
ATT_PAIRS = ATT_HEADS // 2
KV_HEADS = 2
KV_WIDTH = 128
WINDOW = 128
IN_WIDTH = 6400
MIX_WIDTH = 2048
ATT_SCALE = 1.0 / math.sqrt(ATT_DIM)
ROPE_THETA = 10000.0
NORM_EPS = 1e-6
NEG_INF = -1e30
LB_FLOOR = 1e-20

LANES = 128
SUBLANES = 8
CHUNK = 128
LEVELS = 7
VREG_LEVEL = 3
SEQ_TILE = 256
COL_BLOCK = 256
ROW_BLOCK = 64
HEAD_UNROLL = 4
VMEM_LIMIT_BYTES = 56 * 1024 * 1024

OFF_QH, OFF_F, OFF_I, OFF_ZH, OFF_QA, OFF_KV, OFF_ZA = 0, 1024, 2048, 3072, 4096, 5120, 5376

F32 = jnp.float32
BF16 = jnp.bfloat16


def _dot(a, b):
    return jnp.dot(a, b, preferred_element_type=F32)


def _dot_nt(a, b):
    return lax.dot_general(a, b, (((1,), (1,)), ((), ())), preferred_element_type=F32)


def _dot_tn(a, b):
    return lax.dot_general(a, b, (((0,), (0,)), ((), ())), preferred_element_type=F32)


def _silu(x):
    return x * jax.nn.sigmoid(x)


def _rope(x, cos, sin, lo_half):
    swapped = jnp.where(lo_half, pltpu.roll(x, LANES - ATT_DIM // 2, 1), pltpu.roll(x, ATT_DIM // 2, 1))
    return x * cos + swapped * sin


def _hold_rows(b, period, offset):
    pieces = []
    for blk in range(CHUNK // period):
        row = blk * period + offset
        pieces.append(jnp.broadcast_to(b[row:row + 1, :], (period, LANES)))
    return pieces[0] if len(pieces) == 1 else jnp.concatenate(pieces, axis=0)


def _neg_abs(x):
    return pltpu.bitcast(pltpu.bitcast(x, jnp.uint32) | jnp.uint32(0x80000000), F32)


def _level_operands(q, k, b, lf, level, row_idx):
    half = 1 << level
    if level >= VREG_LEVEL:
        left, right = [], []
        for blk in range(CHUNK // (2 * half)):
            lo, mid, hi = blk * 2 * half, blk * 2 * half + half, (blk + 1) * 2 * half
            r = jnp.broadcast_to(b[mid - 1:mid, :], (half, LANES))
            left.append(k[lo:mid] * jnp.exp2(r - b[lo:mid]))
            right.append(q[mid:hi] * jnp.exp2(b[mid:hi] - r))
        both = [piece for pair in zip(left, right) for piece in pair]
        return jnp.concatenate(both, axis=0), (right[0] if len(right) == 1 else jnp.concatenate(right, axis=0))
    if half == 4:
        arg = _neg_abs(b - _hold_rows(b, 8, 3))
    elif half == 2:
        arg = _neg_abs(b - jnp.where((row_idx & 7) < 4, _hold_rows(b, 8, 1), _hold_rows(b, 8, 5)))
    else:
        arg = jnp.where((row_idx & 1) == 1, lf, 0.0)
    is_right = (row_idx & (2 * half - 1)) >= half
    return jnp.where(is_right, q, k) * jnp.exp2(arg), None


def _right_group_index(group, level):
    half_groups = (1 << level) // SUBLANES
    return (group // (2 * half_groups)) * half_groups + group % half_groups


def _layer_kernel(x_ref, win_ref, wout_ref, gpre_ref, gpost_ref, lbf_ref, oml_ref, lbd_ref, ghead_ref,
                  sink_ref, cosq_ref, sinq_ref, cosk_ref, sink_rot_ref, lvl_ref,
                  o_ref,
                  hn_s, qh_s, kh_s, vh_s, lf_s, b_s, zh_s, qa_s, kpad_s, vpad_s, za_s, yin_s, state_s,
                  sc_s, m_s):
    T = SEQ_TILE
    j = pl.program_id(1)

    @pl.when(j == 0)
    def _():
        state_s[...] = jnp.zeros_like(state_s)
        kpad_s[:, 0:CHUNK, :] = jnp.zeros((2 * KV_HEADS, CHUNK, LANES), BF16)
        vpad_s[:, 0:CHUNK, :] = jnp.zeros((2 * KV_HEADS, CHUNK, LANES), BF16)

    for rb in range(T // ROW_BLOCK):
        rows = slice(rb * ROW_BLOCK, (rb + 1) * ROW_BLOCK)
        x = x_ref[0, rows, :]
        ms = jnp.mean(x * x, axis=-1, keepdims=True)
        hn_s[rows, :] = (x * lax.rsqrt(ms + NORM_EPS) * gpre_ref[...]).astype(BF16)

    lane = lax.broadcasted_iota(jnp.int32, (T, LANES), 1)
    lo_half = (lane & (ATT_DIM - 1)) < ATT_DIM // 2
    lo_head = lane < ATT_DIM

    def proj(col):
        return _dot(hn_s[...], win_ref[:, col:col + COL_BLOCK])

    for c in range(HG_WIDTH // COL_BLOCK):
        res = proj(OFF_QH + c * COL_BLOCK)
        qf = _silu(res)
        qh_s[2 * c] = qf[:, :LANES]
        qh_s[2 * c + 1] = qf[:, LANES:]

    for c in range(HG_WIDTH // COL_BLOCK):
        cols = slice(c * COL_BLOCK, (c + 1) * COL_BLOCK)
        res = proj(OFF_F + c * COL_BLOCK)
        sig = jax.nn.sigmoid(res)
        oml = oml_ref[:, cols]
        f = lbf_ref[:, cols] + oml * sig
        lf = jnp.log2(f)
        k = oml * (1.0 - sig) - lbd_ref[:, cols]
        lf_s[2 * c] = lf[:, :LANES]
        lf_s[2 * c + 1] = lf[:, LANES:]
        kh_s[2 * c] = k[:, :LANES]
        kh_s[2 * c + 1] = k[:, LANES:]

    for c in range(HG_WIDTH // COL_BLOCK):
        res = proj(OFF_I + c * COL_BLOCK).astype(BF16)
        vh_s[2 * c] = res[:, :LANES]
        vh_s[2 * c + 1] = res[:, LANES:]

    for c in range(HG_WIDTH // COL_BLOCK):
        res = proj(OFF_ZH + c * COL_BLOCK)
        zh_s[2 * c] = res[:, :LANES]
        zh_s[2 * c + 1] = res[:, LANES:]

    cosq = cosq_ref[...]
    sinq = sinq_ref[...]
    for c in range(ATT_WIDTH // COL_BLOCK):
        res = proj(OFF_QA + c * COL_BLOCK)
        qa_s[2 * c] = _rope(res[:, :LANES], cosq, sinq, lo_half).astype(BF16)
        qa_s[2 * c + 1] = _rope(res[:, LANES:], cosq, sinq, lo_half).astype(BF16)

    res = proj(OFF_KV)
    kr = _rope(res[:, :LANES], cosk_ref[...], sink_rot_ref[...], lo_half)
    kr_sw = pltpu.roll(kr, ATT_DIM, 1)
    va = res[:, LANES:]
    va_sw = pltpu.roll(va, ATT_DIM, 1)
    cur = slice(CHUNK, CHUNK + T)
    zero = jnp.zeros((T, LANES), F32)
    kpad_s[0, cur, :] = jnp.where(lo_head, kr, zero).astype(BF16)
    kpad_s[1, cur, :] = jnp.where(lo_head, zero, kr_sw).astype(BF16)
    kpad_s[2, cur, :] = jnp.where(lo_head, kr_sw, zero).astype(BF16)
    kpad_s[3, cur, :] = jnp.where(lo_head, zero, kr).astype(BF16)
    vpad_s[0, cur, :] = jnp.where(lo_head, va, zero).astype(BF16)
    vpad_s[1, cur, :] = jnp.where(lo_head, zero, va_sw).astype(BF16)
    vpad_s[2, cur, :] = jnp.where(lo_head, va_sw, zero).astype(BF16)
    vpad_s[3, cur, :] = jnp.where(lo_head, zero, va).astype(BF16)

    for c in range(ATT_WIDTH // COL_BLOCK):
        res = proj(OFF_ZA + c * COL_BLOCK)
        za_s[2 * c] = res[:, :LANES]
        za_s[2 * c + 1] = res[:, LANES:]

    row_c = lax.broadcasted_iota(jnp.int32, (CHUNK, LANES), 0)
    col_c = lax.broadcasted_iota(jnp.int32, (CHUNK, LANES), 1)
    tri = (row_c >= col_c).astype(BF16)
    causal = row_c >= col_c
    for ch in range(T // CHUNK):
        rows = slice(ch * CHUNK, (ch + 1) * CHUNK)
        lf = jnp.concatenate([lf_s[h, rows, :] for h in range(HG_HEADS)], axis=1)
        hi = lf.astype(BF16)
        lo = (lf - hi.astype(F32)).astype(BF16)
        b = _dot(tri, hi) + _dot(tri, lo)
        for h in range(HG_HEADS):
            b_s[h, rows, :] = b[:, h * LANES:(h + 1) * LANES]

    lvl = lvl_ref[...]
    ghead = ghead_ref[...]

    def hgrn_head(h, ch):
        rows = pl.ds(ch * CHUNK, CHUNK)
        q = qh_s[h, rows, :]
        k = kh_s[h, rows, :]
        v = vh_s[h, rows, :]
        b = b_s[h, rows, :]
        lf = lf_s[h, rows, :]
        st = state_s[h]

        scores = jnp.zeros((CHUNK, CHUNK), F32)
        for level in range(VREG_LEVEL):
            xk = _level_operands(q, k, b, lf, level, row_c)[0].astype(BF16)
            scores = jnp.where(lvl == level, _dot_nt(xk, xk), scores)
        prods = {}
        for level in range(VREG_LEVEL, LEVELS):
            xk, xr = _level_operands(q, k, b, lf, level, row_c)
            prods[level] = _dot_nt(xr.astype(BF16), xk.astype(BF16))
        groups = []
        for g in range(CHUNK // SUBLANES):
            rs = slice(g * SUBLANES, (g + 1) * SUBLANES)
            acc = scores[rs]
            for level in range(VREG_LEVEL, LEVELS):
                if (g * SUBLANES) & (1 << level):
                    ri = _right_group_index(g, level)
                    acc = jnp.where(lvl[rs] == level, prods[level][ri * SUBLANES:(ri + 1) * SUBLANES], acc)
            groups.append(acc)
        scores = jnp.concatenate(groups, axis=0)

        b_last = b[CHUNK - 1:CHUNK, :]
        q_in = (q * jnp.exp2(b)).astype(BF16)
        o = _dot(scores.astype(BF16), v) + _dot_nt(q_in, st.astype(BF16))
        o = o + jnp.sum(q * k, axis=-1, keepdims=True) * v.astype(F32)

        k_out = (k * jnp.exp2(b_last - b)).astype(BF16)
        state_s[h] = st * jnp.exp2(b_last) + _dot_tn(v, k_out)

        ms = jnp.mean(o * o, axis=-1, keepdims=True)
        on = o * lax.rsqrt(ms + NORM_EPS) * ghead
        yin_s[h, rows, :] = (on * _silu(zh_s[h, rows, :])).astype(BF16)

    for ch in range(T // CHUNK):
        def head_body(h, carry, ch=ch):
            hgrn_head(h, ch)
            return carry
        lax.fori_loop(0, HG_HEADS, head_body, 0, unroll=HEAD_UNROLL)

    ones_blk = jnp.ones((2 * CHUNK, LANES), BF16)
    for n in range(T // CHUNK):
        qrows = slice(n * CHUNK, (n + 1) * CHUNK)
        krows = slice(n * CHUNK, (n + 2) * CHUNK)
        prev_ok = jnp.logical_or(j > 0, n > 0)
        for unit in range(ATT_HEADS):
            p, pos = unit // 2, unit % 2
            g = unit // (ATT_HEADS // KV_HEADS)
            s = _dot_nt(qa_s[p, qrows, :], kpad_s[2 * g + pos, krows, :])
            s_prev = jnp.where(prev_ok, s[:, :CHUNK], NEG_INF)
            sc = jnp.where(causal, s[:, CHUNK:], s_prev)
            sc_s[unit] = sc
            m = jnp.maximum(jnp.max(sc, axis=-1, keepdims=True), sink_ref[unit])
            m_s[unit] = jnp.broadcast_to(m, (CHUNK, LANES))
        for p in range(ATT_PAIRS):
            out = None
            for pos in range(2):
                unit = 2 * p + pos
                g = unit // (ATT_HEADS // KV_HEADS)
                m = m_s[unit]
                pr = jnp.exp(sc_s[unit] - m)
                pcat = jnp.concatenate([jnp.where(causal, 0.0, pr), jnp.where(causal, pr, 0.0)], axis=1)
                vp = jnp.concatenate([vpad_s[2 * g + pos, krows, :], ones_blk], axis=1)
                res = _dot(pcat.astype(BF16), vp)
                denom = res[:, LANES:] + jnp.exp(sink_ref[unit] - m)
                o = res[:, :LANES] / denom
                out = o if out is None else out + o
            yin_s[HG_HEADS + p, qrows, :] = (out * _silu(za_s[p, qrows, :])).astype(BF16)

    kpad_s[:, 0:CHUNK, :] = kpad_s[:, T:T + CHUNK, :]
    vpad_s[:, 0:CHUNK, :] = vpad_s[:, T:T + CHUNK, :]

    yin = jnp.concatenate([yin_s[i] for i in range(MIX_WIDTH // LANES)], axis=1)
    y = _dot(yin, wout_ref[...])
    ms = jnp.mean(y * y, axis=-1, keepdims=True)
    o_ref[0] = x_ref[0] + y * lax.rsqrt(ms + NORM_EPS) * gpost_ref[...]


def _level_matrix():
    t = np.arange(CHUNK)[:, None]
    s = np.arange(CHUNK)[None, :]
    x = np.bitwise_xor(t, s)
    msb = np.where(x > 0, np.floor(np.log2(np.maximum(x, 1))), -1).astype(np.int32)
    return np.where(s < t, msb, -1).astype(np.int32)


def _rope_tables(seq):
    half = ATT_DIM // 2
    inv_freq = ROPE_THETA ** (-jnp.arange(half, dtype=F32) / half)
    ang = jnp.arange(seq, dtype=F32)[:, None] * inv_freq[None, :]
    cos = jnp.tile(jnp.cos(ang), (1, LANES // half))
    sin = jnp.sin(ang)
    sin = jnp.tile(jnp.concatenate([-sin, sin], axis=1), (1, LANES // ATT_DIM))
    return cos, sin


def _layer(x, w_in, w_out, g_pre, g_post, lb, g_head, sinks, tables):
    B, S, _ = x.shape
    T = SEQ_TILE
    cosq, sinq, cosk, sink_rot, lvl = tables
    lbf = jnp.maximum(lb, LB_FLOOR)

    def const(shape):
        return pl.BlockSpec(shape, lambda b, j: (0,) * len(shape))

    def seq_table():
        return pl.BlockSpec((T, LANES), lambda b, j: (j, 0))

    in_specs = [
        pl.BlockSpec((1, T, D_MODEL), lambda b, j: (b, j, 0)),
        const((D_MODEL, IN_WIDTH)),
        const((MIX_WIDTH, D_MODEL)),
        const((1, D_MODEL)),
        const((1, D_MODEL)),
        const((1, HG_WIDTH)),
        const((1, HG_WIDTH)),
        const((1, HG_WIDTH)),
        const((1, HG_DIM)),
        pl.BlockSpec(memory_space=pltpu.SMEM),
        seq_table(), seq_table(), seq_table(), seq_table(),
        const((CHUNK, CHUNK)),
    ]
    scratch = [
        pltpu.VMEM((T, D_MODEL), BF16),
        pltpu.VMEM((HG_HEADS, T, LANES), F32),
        pltpu.VMEM((HG_HEADS, T, LANES), F32),
        pltpu.VMEM((HG_HEADS, T, LANES), BF16),
        pltpu.VMEM((HG_HEADS, T, LANES), F32),
        pltpu.VMEM((HG_HEADS, T, LANES), F32),
        pltpu.VMEM((HG_HEADS, T, LANES), F32),
        pltpu.VMEM((ATT_PAIRS, T, LANES), BF16),
        pltpu.VMEM((2 * KV_HEADS, CHUNK + T, LANES), BF16),
        pltpu.VMEM((2 * KV_HEADS, CHUNK + T, LANES), BF16),
        pltpu.VMEM((ATT_PAIRS, T, LANES), F32),
        pltpu.VMEM((MIX_WIDTH // LANES, T, LANES), BF16),
        pltpu.VMEM((HG_HEADS, HG_DIM, HG_DIM), F32),
        pltpu.VMEM((ATT_HEADS, CHUNK, LANES), F32),
        pltpu.VMEM((ATT_HEADS, CHUNK, LANES), F32),
    ]
    return pl.pallas_call(
        _layer_kernel,
        out_shape=jax.ShapeDtypeStruct(x.shape, x.dtype),
        grid=(B, S // T),
        in_specs=in_specs,
        out_specs=pl.BlockSpec((1, T, D_MODEL), lambda b, j: (b, j, 0)),
        scratch_shapes=scratch,
        compiler_params=pltpu.CompilerParams(
            dimension_semantics=("arbitrary", "arbitrary"),
            vmem_limit_bytes=VMEM_LIMIT_BYTES),
        name="hybrid_layer",
    )(x, w_in.astype(BF16), w_out.astype(BF16), g_pre[None, :], g_post[None, :],
      lbf[None, :], (1.0 - lb)[None, :], (lbf - lb)[None, :], g_head[None, :], sinks,
      cosq, sinq, cosk, sink_rot, lvl)


def kernel(x, w_in, w_out, g_pre, g_post, lb_param, g_head, sinks):
    depth = w_in.shape[0]
    seq = x.shape[1]
    p = jax.nn.softmax(lb_param.astype(F32), axis=0)
    lower_bounds = jnp.cumsum(p, axis=0) - p[0:1]
    cos, sin = _rope_tables(seq)
    tables = (cos * ATT_SCALE, sin * ATT_SCALE, cos, sin, jnp.asarray(_level_matrix()))
    for l in range(depth):
        x = _layer(x, w_in[l], w_out[l], g_pre[l], g_post[l], lower_bounds[l], g_head[l], sinks[l], tables)
    return x
```

```python
import math

import numpy as np
import jax
import jax.numpy as jnp
from jax import lax
from jax.experimental import pallas as pl
from jax.experimental.pallas import tpu as pltpu

D_MODEL = 1024
HG_WIDTH = 1024
HG_HEADS = 8
HG_DIM = 128
ATT_WIDTH = 1024
ATT_HEADS = 16
ATT_DIM = 64
ATT_PAIRS = ATT_HEADS // 2
KV_HEADS = 2
KV_WIDTH = 128
WINDOW = 128
IN_WIDTH = 6400
MIX_WIDTH = 2048
ATT_SCALE = 1.0 / math.sqrt(ATT_DIM)
ROPE_THETA = 10000.0
NORM_EPS = 1e-6
NEG_INF = -1e30
LB_FLOOR = 1e-20

LANES = 128
SUBLANES = 8
CHUNK = 128
HALF = CHUNK // 2
QUARTER = CHUNK // 4
FAST_SPAN_LIMIT = 64.0
LEVELS = 7
VREG_LEVEL = 3
SEQ_TILE = 256
COL_BLOCK = 256
ROW_BLOCK = 64
FAST_HEAD_UNROLL = 8
ROBUST_HEAD_UNROLL = 4
VMEM_LIMIT_BYTES = 56 * 1024 * 1024

OFF_QH, OFF_F, OFF_I, OFF_ZH, OFF_QA, OFF_KV, OFF_ZA = 0, 1024, 2048, 3072, 4096, 5120, 5376

F32 = jnp.float32
BF16 = jnp.bfloat16


def _dot(a, b):
    return jnp.dot(a, b, preferred_element_type=F32)


def _dot_nt(a, b):
    return lax.dot_general(a, b, (((1,), (1,)), ((), ())), preferred_element_type=F32)


def _dot_tn(a, b):
    return lax.dot_general(a, b, (((0,), (0,)), ((), ())), preferred_element_type=F32)


def _silu(x):
    return x * jax.nn.sigmoid(x)


def _rope(x, cos, sin, lo_half):
    swapped = jnp.where(lo_half, pltpu.roll(x, LANES - ATT_DIM // 2, 1), pltpu.roll(x, ATT_DIM // 2, 1))
    return x * cos + swapped * sin


def _hold_rows(b, period, offset):
    pieces = []
    for blk in range(CHUNK // period):
        row = blk * period + offset
        pieces.append(jnp.broadcast_to(b[row:row + 1, :], (period, LANES)))
    return pieces[0] if len(pieces) == 1 else jnp.concatenate(pieces, axis=0)


def _neg_abs(x):
    return -jnp.abs(x)


def _level_operands(q, k, b, lf, level, row_idx):
    half = 1 << level
    if level >= VREG_LEVEL:
        left, right = [], []
        for blk in range(CHUNK // (2 * half)):
            lo, mid, hi = blk * 2 * half, blk * 2 * half + half, (blk + 1) * 2 * half
            r = jnp.broadcast_to(b[mid - 1:mid, :], (half, LANES))
            left.append(k[lo:mid] * jnp.exp2(r - b[lo:mid]))
            right.append(q[mid:hi] * jnp.exp2(b[mid:hi] - r))
        both = [piece for pair in zip(left, right) for piece in pair]
        return jnp.concatenate(both, axis=0), (right[0] if len(right) == 1 else jnp.concatenate(right, axis=0))
    if half == 4:
        arg = _neg_abs(b - _hold_rows(b, 8, 3))
    elif half == 2:
        arg = _neg_abs(b - jnp.where((row_idx & 7) < 4, _hold_rows(b, 8, 1), _hold_rows(b, 8, 5)))
    else:
        arg = jnp.where((row_idx & 1) == 1, lf, 0.0)
    is_right = (row_idx & (2 * half - 1)) >= half
    return jnp.where(is_right, q, k) * jnp.exp2(arg), None


def _right_group_index(group, level):
    half_groups = (1 << level) // SUBLANES
    return (group // (2 * half_groups)) * half_groups + group % half_groups


def _layer_kernel(x_ref, win_ref, wout_ref, gpre_ref, gpost_ref, lbf_ref, oml_ref, lbd_ref, ghead_ref,
                  sink_ref, cosq_ref, sinq_ref, cosk_ref, sink_rot_ref, lvl_ref,
                  o_ref,
                  hn_s, qh_s, kh_s, vh_s, lf_s, b_s, zh_s, qa_s, kpad_s, vpad_s, za_s, yin_s, state_s,
                  sc_s, m_s):
    T = SEQ_TILE
    j = pl.program_id(1)

    @pl.when(j == 0)
    def _():
        state_s[...] = jnp.zeros_like(state_s)
        kpad_s[:, 0:CHUNK, :] = jnp.zeros((2 * KV_HEADS, CHUNK, LANES), BF16)
        vpad_s[:, 0:CHUNK, :] = jnp.zeros((2 * KV_HEADS, CHUNK, LANES), BF16)

    for rb in range(T // ROW_BLOCK):
        rows = slice(rb * ROW_BLOCK, (rb + 1) * ROW_BLOCK)
        x = x_ref[0, rows, :]
        ms = jnp.mean(x * x, axis=-1, keepdims=True)
        hn_s[rows, :] = (x * lax.rsqrt(ms + NORM_EPS) * gpre_ref[...]).astype(BF16)

    lane = lax.broadcasted_iota(jnp.int32, (T, LANES), 1)
    lo_half = (lane & (ATT_DIM - 1)) < ATT_DIM // 2
    lo_head = lane < ATT_DIM

    def proj(col):
        return _dot(hn_s[...], win_ref[:, col:col + COL_BLOCK])

    for c in range(HG_WIDTH // COL_BLOCK):
        res = proj(OFF_QH + c * COL_BLOCK)
        qf = _silu(res)
        qh_s[2 * c] = qf[:, :LANES]
        qh_s[2 * c + 1] = qf[:, LANES:]

    for c in range(HG_WIDTH // COL_BLOCK):
        cols = slice(c * COL_BLOCK, (c + 1) * COL_BLOCK)
        res = proj(OFF_F + c * COL_BLOCK)
        sig = jax.nn.sigmoid(res)
        oml = oml_ref[:, cols]
        f = lbf_ref[:, cols] + oml * sig
        lf = jnp.log2(f)
        k = oml * (1.0 - sig) - lbd_ref[:, cols]
        lf_s[2 * c] = lf[:, :LANES]
        lf_s[2 * c + 1] = lf[:, LANES:]
        kh_s[2 * c] = k[:, :LANES]
        kh_s[2 * c + 1] = k[:, LANES:]

    for c in range(HG_WIDTH // COL_BLOCK):
        res = proj(OFF_I + c * COL_BLOCK).astype(BF16)
        vh_s[2 * c] = res[:, :LANES]
        vh_s[2 * c + 1] = res[:, LANES:]

    for c in range(HG_WIDTH // COL_BLOCK):
        res = proj(OFF_ZH + c * COL_BLOCK)
        zh_s[2 * c] = res[:, :LANES]
        zh_s[2 * c + 1] = res[:, LANES:]

    cosq = cosq_ref[...]
    sinq = sinq_ref[...]
    for c in range(ATT_WIDTH // COL_BLOCK):
        res = proj(OFF_QA + c * COL_BLOCK)
        qa_s[2 * c] = _rope(res[:, :LANES], cosq, sinq, lo_half).astype(BF16)
        qa_s[2 * c + 1] = _rope(res[:, LANES:], cosq, sinq, lo_half).astype(BF16)

    res = proj(OFF_KV)
    kr = _rope(res[:, :LANES], cosk_ref[...], sink_rot_ref[...], lo_half)
    kr_sw = pltpu.roll(kr, ATT_DIM, 1)
    va = res[:, LANES:]
    va_sw = pltpu.roll(va, ATT_DIM, 1)
    cur = slice(CHUNK, CHUNK + T)
    zero = jnp.zeros((T, LANES), F32)
    kpad_s[0, cur, :] = jnp.where(lo_head, kr, zero).astype(BF16)
    kpad_s[1, cur, :] = jnp.where(lo_head, zero, kr_sw).astype(BF16)
    kpad_s[2, cur, :] = jnp.where(lo_head, kr_sw, zero).astype(BF16)
    kpad_s[3, cur, :] = jnp.where(lo_head, zero, kr).astype(BF16)
    vpad_s[0, cur, :] = jnp.where(lo_head, va, zero).astype(BF16)
    vpad_s[1, cur, :] = jnp.where(lo_head, zero, va_sw).astype(BF16)
    vpad_s[2, cur, :] = jnp.where(lo_head, va_sw, zero).astype(BF16)
    vpad_s[3, cur, :] = jnp.where(lo_head, zero, va).astype(BF16)

    for c in range(ATT_WIDTH // COL_BLOCK):
        res = proj(OFF_ZA + c * COL_BLOCK)
        za_s[2 * c] = res[:, :LANES]
        za_s[2 * c + 1] = res[:, LANES:]

    row_c = lax.broadcasted_iota(jnp.int32, (CHUNK, LANES), 0)
    col_c = lax.broadcasted_iota(jnp.int32, (CHUNK, LANES), 1)
    tri = (row_c >= col_c).astype(BF16)
    causal = row_c >= col_c
    worst = None
    for ch in range(T // CHUNK):
        rows = slice(ch * CHUNK, (ch + 1) * CHUNK)
        lf = jnp.concatenate([lf_s[h, rows, :] for h in range(HG_HEADS)], axis=1)
        hi = lf.astype(BF16)
        lo = (lf - hi.astype(F32)).astype(BF16)
        b = _dot(tri, hi) + _dot(tri, lo)
        for h in range(HG_HEADS):
            b_s[h, rows, :] = b[:, h * LANES:(h + 1) * LANES]
        for q0 in range(0, CHUNK, QUARTER):
            first = max(q0 - 1, 0)
            span = b[first:first + 1, :] - b[q0 + QUARTER - 1:q0 + QUARTER, :]
            worst = span if worst is None else jnp.maximum(worst, span)
    fast_ok = jnp.max(worst) <= FAST_SPAN_LIMIT

    lvl = lvl_ref[...]
    ghead = ghead_ref[...]

    def hgrn_load(h, ch):
        rows = pl.ds(ch * CHUNK, CHUNK)
        return rows, qh_s[h, rows, :], kh_s[h, rows, :], vh_s[h, rows, :], b_s[h, rows, :], state_s[h]

    def hgrn_finish(h, rows, q, k, v, b, st, scores, diag):
        b_last = b[CHUNK - 1:CHUNK, :]
        q_in = (q * jnp.exp2(b)).astype(BF16)
        o = _dot(scores.astype(BF16), v) + _dot_nt(q_in, st.astype(BF16))
        if diag is not None:
            o = o + diag * v.astype(F32)
        k_out = (k * jnp.exp2(b_last - b)).astype(BF16)
        state_s[h] = st * jnp.exp2(b_last) + _dot_tn(v, k_out)
        ms = jnp.mean(o * o, axis=-1, keepdims=True)
        on = o * lax.rsqrt(ms + NORM_EPS) * ghead
        yin_s[h, rows, :] = (on * _silu(zh_s[h, rows, :])).astype(BF16)

    def hgrn_head_robust(h, ch):
        rows, q, k, v, b, st = hgrn_load(h, ch)
        lf = lf_s[h, rows, :]
        scores = jnp.zeros((CHUNK, CHUNK), F32)
        for level in range(VREG_LEVEL):
            xk = _level_operands(q, k, b, lf, level, row_c)[0].astype(BF16)
            scores = jnp.where(lvl == level, _dot_nt(xk, xk), scores)
        prods = {}
        for level in range(VREG_LEVEL, LEVELS):
            xk, xr = _level_operands(q, k, b, lf, level, row_c)
            prods[level] = _dot_nt(xr.astype(BF16), xk.astype(BF16))
        groups = []
        for g in range(CHUNK // SUBLANES):
            rs = slice(g * SUBLANES, (g + 1) * SUBLANES)
            acc = scores[rs]
            for level in range(VREG_LEVEL, LEVELS):
                if (g * SUBLANES) & (1 << level):
                    ri = _right_group_index(g, level)
                    acc = jnp.where(lvl[rs] == level, prods[level][ri * SUBLANES:(ri + 1) * SUBLANES], acc)
            groups.append(acc)
        scores = jnp.concatenate(groups, axis=0)
        hgrn_finish(h, rows, q, k, v, b, st, scores, jnp.sum(q * k, axis=-1, keepdims=True))

    def hgrn_head_fast(h, ch):
        rows, q, k, v, b, st = hgrn_load(h, ch)
        mid = jnp.concatenate([jnp.broadcast_to(b[QUARTER - 1:QUARTER, :], (HALF, LANES)),
                               jnp.broadcast_to(b[HALF + QUARTER - 1:HALF + QUARTER, :], (HALF, LANES))], axis=0)
        d = b - mid
        q_own = (q * jnp.exp2(d)).astype(BF16)
        k_own = (k * jnp.exp2(-d)).astype(BF16)
        sep = jnp.broadcast_to(b[HALF - 1:HALF, :], (HALF, LANES))
        q_x = (q[HALF:] * jnp.exp2(b[HALF:] - sep)).astype(BF16)
        k_x = (k[:HALF] * jnp.exp2(sep - b[:HALF])).astype(BF16)
        none = jnp.zeros((HALF, LANES), BF16)
        top = _dot_nt(q_own[:HALF], k_own)
        lhs = jnp.concatenate([q_x, q_own[HALF:]], axis=1)
        rhs = jnp.concatenate([jnp.concatenate([k_x, none], axis=0),
                               jnp.concatenate([none, k_own[HALF:]], axis=0)], axis=1)
        scores = jnp.where(causal, jnp.concatenate([top, _dot_nt(lhs, rhs)], axis=0), 0.0)
        hgrn_finish(h, rows, q, k, v, b, st, scores, None)

    def hgrn_all(head_fn, unroll):
        for ch in range(T // CHUNK):
            def head_body(h, carry, ch=ch):
                head_fn(h, ch)
                return carry
            lax.fori_loop(0, HG_HEADS, head_body, 0, unroll=unroll)

    @pl.when(fast_ok)
    def _():
        hgrn_all(hgrn_head_fast, FAST_HEAD_UNROLL)

    @pl.when(jnp.logical_not(fast_ok))
    def _():
        hgrn_all(hgrn_head_robust, ROBUST_HEAD_UNROLL)

    ones_blk = jnp.ones((2 * CHUNK, LANES), BF16)
    for n in range(T // CHUNK):
        qrows = slice(n * CHUNK, (n + 1) * CHUNK)
        krows = slice(n * CHUNK, (n + 2) * CHUNK)
        prev_ok = jnp.logical_or(j > 0, n > 0)
        for unit in range(ATT_HEADS):
            p, pos = unit // 2, unit % 2
            g = unit // (ATT_HEADS // KV_HEADS)
            s = _dot_nt(qa_s[p, qrows, :], kpad_s[2 * g + pos, krows, :])
            s_prev = jnp.where(prev_ok, s[:, :CHUNK], NEG_INF)
            sc = jnp.where(causal, s[:, CHUNK:], s_prev)
            sc_s[unit] = sc
            m = jnp.maximum(jnp.max(sc, axis=-1, keepdims=True), sink_ref[unit])
            m_s[unit] = jnp.broadcast_to(m, (CHUNK, LANES))
        for p in range(ATT_PAIRS):
            out = None
            for pos in range(2):
                unit = 2 * p + pos
                g = unit // (ATT_HEADS // KV_HEADS)
                m = m_s[unit]
                pr = jnp.exp(sc_s[unit] - m)
                pcat = jnp.concatenate([jnp.where(causal, 0.0, pr), jnp.where(causal, pr, 0.0)], axis=1)
                vp = jnp.concatenate([vpad_s[2 * g + pos, krows, :], ones_blk], axis=1)
                res = _dot(pcat.astype(BF16), vp)
                denom = res[:, LANES:] + jnp.exp(sink_ref[unit] - m)
                o = res[:, :LANES] / denom
                out = o if out is None else out + o
            yin_s[HG_HEADS + p, qrows, :] = (out * _silu(za_s[p, qrows, :])).astype(BF16)

    kpad_s[:, 0:CHUNK, :] = kpad_s[:, T:T + CHUNK, :]
    vpad_s[:, 0:CHUNK, :] = vpad_s[:, T:T + CHUNK, :]

    yin = jnp.concatenate([yin_s[i] for i in range(MIX_WIDTH // LANES)], axis=1)
    y = _dot(yin, wout_ref[...])
    ms = jnp.mean(y * y, axis=-1, keepdims=True)
    o_ref[0] = x_ref[0] + y * lax.rsqrt(ms + NORM_EPS) * gpost_ref[...]


def _level_matrix():
    t = np.arange(CHUNK)[:, None]
    s = np.arange(CHUNK)[None, :]
    x = np.bitwise_xor(t, s)
    msb = np.where(x > 0, np.floor(np.log2(np.maximum(x, 1))), -1).astype(np.int32)
    return np.where(s < t, msb, -1).astype(np.int32)


def _rope_tables(seq):
    half = ATT_DIM // 2
    inv_freq = ROPE_THETA ** (-jnp.arange(half, dtype=F32) / half)
    ang = jnp.arange(seq, dtype=F32)[:, None] * inv_freq[None, :]
    cos = jnp.tile(jnp.cos(ang), (1, LANES // half))
    sin = jnp.sin(ang)
    sin = jnp.tile(jnp.concatenate([-sin, sin], axis=1), (1, LANES // ATT_DIM))
    return cos, sin


def _layer(x, w_in, w_out, g_pre, g_post, lb, g_head, sinks, tables):
    B, S, _ = x.shape
    T = SEQ_TILE
    cosq, sinq, cosk, sink_rot, lvl = tables
    lbf = jnp.maximum(lb, LB_FLOOR)

    def const(shape):
        return pl.BlockSpec(shape, lambda b, j: (0,) * len(shape))

    def seq_table():
        return pl.BlockSpec((T, LANES), lambda b, j: (j, 0))

    in_specs = [
        pl.BlockSpec((1, T, D_MODEL), lambda b, j: (b, j, 0)),
        const((D_MODEL, IN_WIDTH)),
        const((MIX_WIDTH, D_MODEL)),
        const((1, D_MODEL)),
        const((1, D_MODEL)),
        const((1, HG_WIDTH)),
        const((1, HG_WIDTH)),
        const((1, HG_WIDTH)),
        const((1, HG_DIM)),
        pl.BlockSpec(memory_space=pltpu.SMEM),
        seq_table(), seq_table(), seq_table(), seq_table(),
        const((CHUNK, CHUNK)),
    ]
    scratch = [
        pltpu.VMEM((T, D_MODEL), BF16),
        pltpu.VMEM((HG_HEADS, T, LANES), F32),
        pltpu.VMEM((HG_HEADS, T, LANES), F32),
        pltpu.VMEM((HG_HEADS, T, LANES), BF16),
        pltpu.VMEM((HG_HEADS, T, LANES), F32),
        pltpu.VMEM((HG_HEADS, T, LANES), F32),
        pltpu.VMEM((HG_HEADS, T, LANES), F32),
        pltpu.VMEM((ATT_PAIRS, T, LANES), BF16),
        pltpu.VMEM((2 * KV_HEADS, CHUNK + T, LANES), BF16),
        pltpu.VMEM((2 * KV_HEADS, CHUNK + T, LANES), BF16),
        pltpu.VMEM((ATT_PAIRS, T, LANES), F32),
        pltpu.VMEM((MIX_WIDTH // LANES, T, LANES), BF16),
        pltpu.VMEM((HG_HEADS, HG_DIM, HG_DIM), F32),
        pltpu.VMEM((ATT_HEADS, CHUNK, LANES), F32),
        pltpu.VMEM((ATT_HEADS, CHUNK, LANES), F32),
    ]
    return pl.pallas_call(
        _layer_kernel,
        out_shape=jax.ShapeDtypeStruct(x.shape, x.dtype),
        grid=(B, S // T),
        in_specs=in_specs,
        out_specs=pl.BlockSpec((1, T, D_MODEL), lambda b, j: (b, j, 0)),
        scratch_shapes=scratch,
        compiler_params=pltpu.CompilerParams(
            dimension_semantics=("arbitrary", "arbitrary"),
            vmem_limit_bytes=VMEM_LIMIT_BYTES),
        name="hybrid_layer",
    )(x, w_in.astype(BF16), w_out.astype(BF16), g_pre[None, :], g_post[None, :],
      lbf[None, :], (1.0 - lb)[None, :], (lbf - lb)[None, :], g_head[None, :], sinks,
      cosq, sinq, cosk, sink_rot, lvl)


def kernel(x, w_in, w_out, g_pre, g_post, lb_param, g_head, sinks):
    depth = w_in.shape[0]
    seq = x.shape[1]
    p = jax.nn.softmax(lb_param.astype(F32), axis=0)
    lower_bounds = jnp.cumsum(p, axis=0) - p[0:1]
    cos, sin = _rope_tables(seq)
    tables = (cos * ATT_SCALE, sin * ATT_SCALE, cos, sin, jnp.asarray(_level_matrix()))
    for l in range(depth):
        x = _layer(x, w_in[l], w_out[l], g_pre[l], g_post[l], lower_bounds[l], g_head[l], sinks[l], tables)
    return x
```

```python
import math

import numpy as np
import jax
import jax.numpy as jnp
from jax import lax
from jax.experimental import pallas as pl
from jax.experimental.pallas import tpu as pltpu

D_MODEL = 1024
HG_WIDTH = 1024
HG_HEADS = 8
HG_DIM = 128
ATT_WIDTH = 1024
ATT_HEADS = 16
ATT_DIM = 64
ATT_PAIRS = ATT_HEADS // 2
KV_HEADS = 2
KV_WIDTH = 128
PAIRS_PER_KV = ATT_PAIRS // KV_HEADS
WINDOW = 128
IN_WIDTH = 6400
MIX_WIDTH = 2048
ATT_SCALE = 1.0 / math.sqrt(ATT_DIM)
ROPE_THETA = 10000.0
NORM_EPS = 1e-6
NEG_INF = -1e30
LB_FLOOR = 1e-20

LANES = 128
SUBLANES = 8
CHUNK = 128
HALF = CHUNK // 2
QUARTER = CHUNK // 4
FAST_SPAN_LIMIT = 64.0
LEVELS = 7
VREG_LEVEL = 3
SEQ_TILE = 256
COL_BLOCK = 256
ROW_BLOCK = 64
ROBUST_HEAD_UNROLL = 4
VMEM_LIMIT_BYTES = 56 * 1024 * 1024

OFF_QH, OFF_F, OFF_I, OFF_ZH, OFF_QA, OFF_KV, OFF_ZA = 0, 1024, 2048, 3072, 4096, 5120, 5376

F32 = jnp.float32
BF16 = jnp.bfloat16


def _dot(a, b):
    return jnp.dot(a, b, preferred_element_type=F32)


def _dot_nt(a, b):
    return lax.dot_general(a, b, (((1,), (1,)), ((), ())), preferred_element_type=F32)


def _dot_tn(a, b):
    return lax.dot_general(a, b, (((0,), (0,)), ((), ())), preferred_element_type=F32)


def _silu(x):
    return x * jax.nn.sigmoid(x)


def _rope(x, cos, sin, lo_half):
    swapped = jnp.where(lo_half, pltpu.roll(x, LANES - ATT_DIM // 2, 1), pltpu.roll(x, ATT_DIM // 2, 1))
    return x * cos + swapped * sin


def _hold_rows(b, period, offset):
    pieces = []
    for blk in range(CHUNK // period):
        row = blk * period + offset
        pieces.append(jnp.broadcast_to(b[row:row + 1, :], (period, LANES)))
    return pieces[0] if len(pieces) == 1 else jnp.concatenate(pieces, axis=0)


def _neg_abs(x):
    return -jnp.abs(x)


def _level_operands(q, k, b, lf, level, row_idx):
    half = 1 << level
    if level >= VREG_LEVEL:
        left, right = [], []
        for blk in range(CHUNK // (2 * half)):
            lo, mid, hi = blk * 2 * half, blk * 2 * half + half, (blk + 1) * 2 * half
            r = jnp.broadcast_to(b[mid - 1:mid, :], (half, LANES))
            left.append(k[lo:mid] * jnp.exp2(r - b[lo:mid]))
            right.append(q[mid:hi] * jnp.exp2(b[mid:hi] - r))
        both = [piece for pair in zip(left, right) for piece in pair]
        return jnp.concatenate(both, axis=0), (right[0] if len(right) == 1 else jnp.concatenate(right, axis=0))
    if half == 4:
        arg = _neg_abs(b - _hold_rows(b, 8, 3))
    elif half == 2:
        arg = _neg_abs(b - jnp.where((row_idx & 7) < 4, _hold_rows(b, 8, 1), _hold_rows(b, 8, 5)))
    else:
        arg = jnp.where((row_idx & 1) == 1, lf, 0.0)
    is_right = (row_idx & (2 * half - 1)) >= half
    return jnp.where(is_right, q, k) * jnp.exp2(arg), None


def _right_group_index(group, level):
    half_groups = (1 << level) // SUBLANES
    return (group // (2 * half_groups)) * half_groups + group % half_groups


def _layer_kernel(x_ref, win_ref, wout_ref, gpre_ref, gpost_ref, lbf_ref, oml_ref, lbd_ref, ghead_ref,
                  sink_ref, cosq_ref, sinq_ref, cosk_ref, sink_rot_ref, lvl_ref, tri_ref,
                  o_ref,
                  hn_s, qh_s, kh_s, vh_s, lf_s, b_s, zh_s, qa_s, kpad_s, vpad_s, za_s, yin_s, state_s,
                  sc_s, m_s, stbk_s, y_s, pst_s):
    T = SEQ_TILE
    j = pl.program_id(1)

    @pl.when(j == 0)
    def _():
        state_s[...] = jnp.zeros_like(state_s)
        kpad_s[:, 0:CHUNK, :] = jnp.zeros((2 * KV_HEADS, CHUNK, LANES), BF16)
        vpad_s[:, 0:CHUNK, :] = jnp.zeros((2 * KV_HEADS, CHUNK, LANES), BF16)

    lane = lax.broadcasted_iota(jnp.int32, (T, LANES), 1)
    lo_half = (lane & (ATT_DIM - 1)) < ATT_DIM // 2
    lo_head = lane < ATT_DIM
    row_c = lax.broadcasted_iota(jnp.int32, (CHUNK, LANES), 0)
    col_c = lax.broadcasted_iota(jnp.int32, (CHUNK, LANES), 1)
    causal = row_c >= col_c
    lvl = lvl_ref[...]
    ghead = ghead_ref[...]
    ones_blk = jnp.ones((2 * CHUNK, LANES), BF16)
    worst_spans = []

    def pre_norm():
        for rb in range(T // ROW_BLOCK):
            rows = slice(rb * ROW_BLOCK, (rb + 1) * ROW_BLOCK)
            x = x_ref[0, rows, :]
            ms = jnp.mean(x * x, axis=-1, keepdims=True)
            hn_s[rows, :] = (x * lax.rsqrt(ms + NORM_EPS) * gpre_ref[...]).astype(BF16)

    def proj(col):
        return _dot(hn_s[...], win_ref[:, col:col + COL_BLOCK])

    def proj_q(c):
        qf = _silu(proj(OFF_QH + c * COL_BLOCK))
        qh_s[2 * c] = qf[:, :LANES]
        qh_s[2 * c + 1] = qf[:, LANES:]

    def proj_f(c):
        cols = slice(c * COL_BLOCK, (c + 1) * COL_BLOCK)
        sig = jax.nn.sigmoid(proj(OFF_F + c * COL_BLOCK))
        oml = oml_ref[:, cols]
        lf = jnp.log2(lbf_ref[:, cols] + oml * sig)
        k = oml * (1.0 - sig) - lbd_ref[:, cols]
        lf_s[2 * c] = lf[:, :LANES]
        lf_s[2 * c + 1] = lf[:, LANES:]
        kh_s[2 * c] = k[:, :LANES]
        kh_s[2 * c + 1] = k[:, LANES:]

    def proj_i(c):
        res = proj(OFF_I + c * COL_BLOCK).astype(BF16)
        vh_s[2 * c] = res[:, :LANES]
        vh_s[2 * c + 1] = res[:, LANES:]

    def proj_z(c):
        res = proj(OFF_ZH + c * COL_BLOCK)
        zh_s[2 * c] = res[:, :LANES]
        zh_s[2 * c + 1] = res[:, LANES:]

    def proj_qa(c):
        res = proj(OFF_QA + c * COL_BLOCK)
        qa_s[2 * c] = _rope(res[:, :LANES], cosq_ref[...], sinq_ref[...], lo_half).astype(BF16)
        qa_s[2 * c + 1] = _rope(res[:, LANES:], cosq_ref[...], sinq_ref[...], lo_half).astype(BF16)

    def proj_kv():
        res = proj(OFF_KV)
        kr = _rope(res[:, :LANES], cosk_ref[...], sink_rot_ref[...], lo_half)
        kr_sw = pltpu.roll(kr, ATT_DIM, 1)
        va = res[:, LANES:]
        va_sw = pltpu.roll(va, ATT_DIM, 1)
        cur = slice(CHUNK, CHUNK + T)
        zero = jnp.zeros((T, LANES), F32)
        kpad_s[0, cur, :] = jnp.where(lo_head, kr, zero).astype(BF16)
        kpad_s[1, cur, :] = jnp.where(lo_head, zero, kr_sw).astype(BF16)
        kpad_s[2, cur, :] = jnp.where(lo_head, kr_sw, zero).astype(BF16)
        kpad_s[3, cur, :] = jnp.where(lo_head, zero, kr).astype(BF16)
        vpad_s[0, cur, :] = jnp.where(lo_head, va, zero).astype(BF16)
        vpad_s[1, cur, :] = jnp.where(lo_head, zero, va_sw).astype(BF16)
        vpad_s[2, cur, :] = jnp.where(lo_head, va_sw, zero).astype(BF16)
        vpad_s[3, cur, :] = jnp.where(lo_head, zero, va).astype(BF16)

    def proj_za(c):
        res = proj(OFF_ZA + c * COL_BLOCK)
        za_s[2 * c] = res[:, :LANES]
        za_s[2 * c + 1] = res[:, LANES:]

    def cumsum_pair(c):
        for ch in range(T // CHUNK):
            rows = slice(ch * CHUNK, (ch + 1) * CHUNK)
            lf = jnp.concatenate([lf_s[2 * c, rows, :], lf_s[2 * c + 1, rows, :]], axis=1)
            hi = lf.astype(BF16)
            lo = (lf - hi.astype(F32)).astype(BF16)
            b = _dot(tri_ref[...], hi) + _dot(tri_ref[...], lo)
            b_s[2 * c, rows, :] = b[:, :LANES]
            b_s[2 * c + 1, rows, :] = b[:, LANES:]
            for q0 in range(0, CHUNK, QUARTER):
                first = max(q0 - 1, 0)
                worst_spans.append(b[first:first + 1, :] - b[q0 + QUARTER - 1:q0 + QUARTER, :])

    def hgrn_load(h, ch):
        rows = pl.ds(ch * CHUNK, CHUNK)
        return rows, qh_s[h, rows, :], kh_s[h, rows, :], vh_s[h, rows, :], b_s[h, rows, :], state_s[h]

    def hgrn_finish(h, rows, q, k, v, b, st, scores, diag):
        b_last = b[CHUNK - 1:CHUNK, :]
        q_in = (q * jnp.exp2(b)).astype(BF16)
        o = _dot(scores.astype(BF16), v) + _dot_nt(q_in, st.astype(BF16))
        if diag is not None:
            o = o + diag * v.astype(F32)
        k_out = (k * jnp.exp2(b_last - b)).astype(BF16)
        state_s[h] = st * jnp.exp2(b_last) + _dot_tn(v, k_out)
        ms = jnp.mean(o * o, axis=-1, keepdims=True)
        on = o * lax.rsqrt(ms + NORM_EPS) * ghead
        yin_s[h, rows, :] = (on * _silu(zh_s[h, rows, :])).astype(BF16)

    def hgrn_head_robust(h, ch):
        rows, q, k, v, b, st = hgrn_load(h, ch)
        lf = lf_s[h, rows, :]
        scores = jnp.zeros((CHUNK, CHUNK), F32)
        for level in range(VREG_LEVEL):
            xk = _level_operands(q, k, b, lf, level, row_c)[0].astype(BF16)
            scores = jnp.where(lvl == level, _dot_nt(xk, xk), scores)
        prods = {}
        for level in range(VREG_LEVEL, LEVELS):
            xk, xr = _level_operands(q, k, b, lf, level, row_c)
            prods[level] = _dot_nt(xr.astype(BF16), xk.astype(BF16))
        groups = []
        for g in range(CHUNK // SUBLANES):
            rs = slice(g * SUBLANES, (g + 1) * SUBLANES)
            acc = scores[rs]
            for level in range(VREG_LEVEL, LEVELS):
                if (g * SUBLANES) & (1 << level):
                    ri = _right_group_index(g, level)
                    acc = jnp.where(lvl[rs] == level, prods[level][ri * SUBLANES:(ri + 1) * SUBLANES], acc)
            groups.append(acc)
        scores = jnp.concatenate(groups, axis=0)
        hgrn_finish(h, rows, q, k, v, b, st, scores, jnp.sum(q * k, axis=-1, keepdims=True))

    def hgrn_head_fast(h, ch):
        rows, q, k, v, b, st = hgrn_load(h, ch)
        mid = jnp.concatenate([jnp.broadcast_to(b[QUARTER - 1:QUARTER, :], (HALF, LANES)),
                               jnp.broadcast_to(b[HALF + QUARTER - 1:HALF + QUARTER, :], (HALF, LANES))], axis=0)
        d = b - mid
        q_own = (q * jnp.exp2(d)).astype(BF16)
        k_own = (k * jnp.exp2(-d)).astype(BF16)
        sep = jnp.broadcast_to(b[HALF - 1:HALF, :], (HALF, LANES))
        q_x = (q[HALF:] * jnp.exp2(b[HALF:] - sep)).astype(BF16)
        k_x = (k[:HALF] * jnp.exp2(sep - b[:HALF])).astype(BF16)
        none = jnp.zeros((HALF, LANES), BF16)
        top = _dot_nt(q_own[:HALF], k_own)
        lhs = jnp.concatenate([q_x, q_own[HALF:]], axis=1)
        rhs = jnp.concatenate([jnp.concatenate([k_x, none], axis=0),
                               jnp.concatenate([none, k_own[HALF:]], axis=0)], axis=1)
        scores = jnp.where(causal, jnp.concatenate([top, _dot_nt(lhs, rhs)], axis=0), 0.0)
        hgrn_finish(h, rows, q, k, v, b, st, scores, None)

    def attn_scores(n, g, pos):
        qrows = slice(n * CHUNK, (n + 1) * CHUNK)
        krows = slice(n * CHUNK, (n + 2) * CHUNK)
        pairs = range(g * PAIRS_PER_KV, (g + 1) * PAIRS_PER_KV)
        q_stack = jnp.concatenate([qa_s[p, qrows, :] for p in pairs], axis=0)
        s_all = _dot_nt(q_stack, kpad_s[2 * g + pos, krows, :])
        for i, p in enumerate(pairs):
            head = 2 * p + pos
            s = s_all[i * CHUNK:(i + 1) * CHUNK]
            s_prev = jnp.where(jnp.logical_or(j > 0, n > 0), s[:, :CHUNK], NEG_INF)
            sc = jnp.where(causal, s[:, CHUNK:], s_prev)
            sc_s[n % 2 * ATT_HEADS + head] = sc
            m = jnp.maximum(jnp.max(sc, axis=-1, keepdims=True), sink_ref[head])
            m_s[n % 2 * ATT_HEADS + head] = jnp.broadcast_to(m, (CHUNK, LANES))

    def attn_out(n, g):
        qrows = slice(n * CHUNK, (n + 1) * CHUNK)
        krows = slice(n * CHUNK, (n + 2) * CHUNK)
        pairs = range(g * PAIRS_PER_KV, (g + 1) * PAIRS_PER_KV)
        res = []
        for pos in range(2):
            for i, p in enumerate(pairs):
                head = 2 * p + pos
                pr = jnp.exp(sc_s[n % 2 * ATT_HEADS + head] - m_s[n % 2 * ATT_HEADS + head])
                pcat = jnp.concatenate([jnp.where(causal, 0.0, pr), jnp.where(causal, pr, 0.0)], axis=1)
                pst_s[pos, i * CHUNK:(i + 1) * CHUNK, :] = pcat.astype(BF16)
            vp = jnp.concatenate([vpad_s[2 * g + pos, krows, :], ones_blk], axis=1)
            res.append(_dot(pst_s[pos], vp))
        for i, p in enumerate(pairs):
            out = None
            for pos in range(2):
                head = 2 * p + pos
                r = res[pos][i * CHUNK:(i + 1) * CHUNK]
                denom = r[:, LANES:] + jnp.exp(sink_ref[head] - m_s[n % 2 * ATT_HEADS + head])
                o = r[:, :LANES] / denom
                out = o if out is None else out + o
            yin_s[HG_HEADS + p, qrows, :] = (out * _silu(za_s[p, qrows, :])).astype(BF16)

    def out_proj_hgrn(nb):
        cols = slice(nb * COL_BLOCK, (nb + 1) * COL_BLOCK)
        yin = jnp.concatenate([yin_s[i] for i in range(HG_HEADS)], axis=1)
        y_s[:, cols] = _dot(yin, wout_ref[0:HG_WIDTH, cols])

    def out_proj_attn(nb):
        cols = slice(nb * COL_BLOCK, (nb + 1) * COL_BLOCK)
        yin = jnp.concatenate([yin_s[HG_HEADS + i] for i in range(ATT_PAIRS)], axis=1)
        y_s[:, cols] = y_s[:, cols] + _dot(yin, wout_ref[HG_WIDTH:MIX_WIDTH, cols])

    def post_norm():
        for rb in range(T // ROW_BLOCK):
            rows = slice(rb * ROW_BLOCK, (rb + 1) * ROW_BLOCK)
            y = y_s[rows, :]
            ms = jnp.mean(y * y, axis=-1, keepdims=True)
            o_ref[0, rows, :] = x_ref[0, rows, :] + y * lax.rsqrt(ms + NORM_EPS) * gpost_ref[...]

    n_chunks = T // CHUNK
    n_pairs = HG_HEADS // 2
    assert ATT_WIDTH // COL_BLOCK == 2 * KV_HEADS and (n_chunks * KV_HEADS) % (D_MODEL // COL_BLOCK) == 0
    pre_norm()
    stbk_s[...] = state_s[...]
    for c in range(n_pairs + 1):
        if c < n_pairs:
            proj_steps = [lambda c=c: proj_q(c), lambda c=c: proj_f(c), lambda c=c: proj_i(c), lambda c=c: proj_z(c)]
        else:
            proj_steps = [proj_kv] + [lambda a=a: proj_qa(a) for a in range(ATT_WIDTH // COL_BLOCK)]
        mixer_steps = []
        if c > 0:
            mixer_steps = [lambda h=h, ch=ch: hgrn_head_fast(h, ch)
                           for ch in range(n_chunks) for h in (2 * c - 2, 2 * c - 1)]
        for i in range(max(len(proj_steps), len(mixer_steps))):
            if i < len(proj_steps):
                proj_steps[i]()
            if i < len(mixer_steps):
                mixer_steps[i]()
        if c < n_pairs:
            cumsum_pair(c)
    attn_units = [(g, pos) for g in range(KV_HEADS) for pos in range(2)]
    for a in range(ATT_WIDTH // COL_BLOCK):
        proj_za(a)
        attn_scores(0, *attn_units[a])
    out_every = n_chunks * KV_HEADS // (D_MODEL // COL_BLOCK)
    for n in range(n_chunks):
        for g in range(KV_HEADS):
            attn_out(n, g)
            if n + 1 < n_chunks:
                attn_scores(n + 1, g, 0)
                attn_scores(n + 1, g, 1)
            item = n * KV_HEADS + g
            if item % out_every == out_every - 1:
                out_proj_hgrn(item // out_every)

    worst = worst_spans[0]
    for span in worst_spans[1:]:
        worst = jnp.maximum(worst, span)
    fast_ok = jnp.max(worst) <= FAST_SPAN_LIMIT

    @pl.when(jnp.logical_not(fast_ok))
    def _():
        state_s[...] = stbk_s[...]
        for ch in range(n_chunks):
            def head_body(h, carry, ch=ch):
                hgrn_head_robust(h, ch)
                return carry
            lax.fori_loop(0, HG_HEADS, head_body, 0, unroll=ROBUST_HEAD_UNROLL)
        for nb in range(D_MODEL // COL_BLOCK):
            out_proj_hgrn(nb)

    kpad_s[:, 0:CHUNK, :] = kpad_s[:, T:T + CHUNK, :]
    vpad_s[:, 0:CHUNK, :] = vpad_s[:, T:T + CHUNK, :]

    for nb in range(D_MODEL // COL_BLOCK):
        out_proj_attn(nb)
    post_norm()


def _level_matrix():
    t = np.arange(CHUNK)[:, None]
    s = np.arange(CHUNK)[None, :]
    x = np.bitwise_xor(t, s)
    msb = np.where(x > 0, np.floor(np.log2(np.maximum(x, 1))), -1).astype(np.int32)
    return np.where(s < t, msb, -1).astype(np.int32)


def _rope_tables(seq):
    half = ATT_DIM // 2
    inv_freq = ROPE_THETA ** (-jnp.arange(half, dtype=F32) / half)
    ang = jnp.arange(seq, dtype=F32)[:, None] * inv_freq[None, :]
    cos = jnp.tile(jnp.cos(ang), (1, LANES // half))
    sin = jnp.sin(ang)
    sin = jnp.tile(jnp.concatenate([-sin, sin], axis=1), (1, LANES // ATT_DIM))
    return cos, sin


def _layer(x, w_in, w_out, g_pre, g_post, lb, g_head, sinks, tables):
    B, S, _ = x.shape
    T = SEQ_TILE
    cosq, sinq, cosk, sink_rot, lvl, tri = tables
    lbf = jnp.maximum(lb, LB_FLOOR)

    def const(shape):
        return pl.BlockSpec(shape, lambda b, j: (0,) * len(shape), pipeline_mode=pl.Buffered(1))

    def seq_table():
        return pl.BlockSpec((T, LANES), lambda b, j: (j, 0))

    in_specs = [
        pl.BlockSpec((1, T, D_MODEL), lambda b, j: (b, j, 0)),
        const((D_MODEL, IN_WIDTH)),
        const((MIX_WIDTH, D_MODEL)),
        const((1, D_MODEL)),
        const((1, D_MODEL)),
        const((1, HG_WIDTH)),
        const((1, HG_WIDTH)),
        const((1, HG_WIDTH)),
        const((1, HG_DIM)),
        pl.BlockSpec(memory_space=pltpu.SMEM),
        seq_table(), seq_table(), seq_table(), seq_table(),
        const((CHUNK, CHUNK)),
        const((CHUNK, CHUNK)),
    ]
    scratch = [
        pltpu.VMEM((T, D_MODEL), BF16),
        pltpu.VMEM((HG_HEADS, T, LANES), F32),
        pltpu.VMEM((HG_HEADS, T, LANES), F32),
        pltpu.VMEM((HG_HEADS, T, LANES), BF16),
        pltpu.VMEM((HG_HEADS, T, LANES), F32),
        pltpu.VMEM((HG_HEADS, T, LANES), F32),
        pltpu.VMEM((HG_HEADS, T, LANES), F32),
        pltpu.VMEM((ATT_PAIRS, T, LANES), BF16),
        pltpu.VMEM((2 * KV_HEADS, CHUNK + T, LANES), BF16),
        pltpu.VMEM((2 * KV_HEADS, CHUNK + T, LANES), BF16),
        pltpu.VMEM((ATT_PAIRS, T, LANES), F32),
        pltpu.VMEM((MIX_WIDTH // LANES, T, LANES), BF16),
        pltpu.VMEM((HG_HEADS, HG_DIM, HG_DIM), F32),
        pltpu.VMEM((2 * ATT_HEADS, CHUNK, LANES), F32),
        pltpu.VMEM((2 * ATT_HEADS, CHUNK, LANES), F32),
        pltpu.VMEM((HG_HEADS, HG_DIM, HG_DIM), F32),
        pltpu.VMEM((T, D_MODEL), F32),
        pltpu.VMEM((2, PAIRS_PER_KV * CHUNK, 2 * CHUNK), BF16),
    ]
    return pl.pallas_call(
        _layer_kernel,
        out_shape=jax.ShapeDtypeStruct(x.shape, x.dtype),
        grid=(B, S // T),
        in_specs=in_specs,
        out_specs=pl.BlockSpec((1, T, D_MODEL), lambda b, j: (b, j, 0)),
        scratch_shapes=scratch,
        compiler_params=pltpu.CompilerParams(
            dimension_semantics=("arbitrary", "arbitrary"),
            vmem_limit_bytes=VMEM_LIMIT_BYTES),
        name="hybrid_layer",
    )(x, w_in.astype(BF16), w_out.astype(BF16), g_pre[None, :], g_post[None, :],
      lbf[None, :], (1.0 - lb)[None, :], (lbf - lb)[None, :], g_head[None, :], sinks,
      cosq, sinq, cosk, sink_rot, lvl, tri)


def kernel(x, w_in, w_out, g_pre, g_post, lb_param, g_head, sinks):
    depth = w_in.shape[0]
    seq = x.shape[1]
    p = jax.nn.softmax(lb_param.astype(F32), axis=0)
    lower_bounds = jnp.cumsum(p, axis=0) - p[0:1]
    cos, sin = _rope_tables(seq)
    tri = jnp.asarray(np.tril(np.ones((CHUNK, CHUNK), np.float32)), BF16)
    tables = (cos * ATT_SCALE, sin * ATT_SCALE, cos, sin, jnp.asarray(_level_matrix()), tri)
    for l in range(depth):
        x = _layer(x, w_in[l], w_out[l], g_pre[l], g_post[l], lower_bounds[l], g_head[l], sinks[l], tables)
    return x
```

```python
import math

import numpy as np
import jax
import jax.numpy as jnp
from jax import lax
from jax.experimental import pallas as pl
from jax.experimental.pallas import tpu as pltpu

D_MODEL = 1024
HG_WIDTH = 1024
HG_HEADS = 8
HG_DIM = 128
ATT_WIDTH = 1024
ATT_HEADS = 16
ATT_DIM = 64
ATT_PAIRS = ATT_HEADS // 2
KV_HEADS = 2
KV_WIDTH = 128
PAIRS_PER_KV = ATT_PAIRS // KV_HEADS
WINDOW = 128
IN_WIDTH = 6400
MIX_WIDTH = 2048
ATT_SCALE = 1.0 / math.sqrt(ATT_DIM)
ROPE_THETA = 10000.0
NORM_EPS = 1e-6
NEG_INF = -1e30
LB_FLOOR = 1e-20

LANES = 128
SUBLANES = 8
CHUNK = 128
HALF = CHUNK // 2
QUARTER = CHUNK // 4
FAST_SPAN_LIMIT = 64.0
LEVELS = 7
VREG_LEVEL = 3
SEQ_TILE = 512
COL_BLOCK = 256
ROW_BLOCK = 64
ROBUST_HEAD_UNROLL = 4
VMEM_LIMIT_BYTES = 56 * 1024 * 1024

OFF_QH, OFF_F, OFF_I, OFF_ZH, OFF_QA, OFF_KV, OFF_ZA = 0, 1024, 2048, 3072, 4096, 5120, 5376

F32 = jnp.float32
BF16 = jnp.bfloat16


def _dot(a, b):
    return jnp.dot(a, b, preferred_element_type=F32)


def _dot_nt(a, b):
    return lax.dot_general(a, b, (((1,), (1,)), ((), ())), preferred_element_type=F32)


def _dot_tn(a, b):
    return lax.dot_general(a, b, (((0,), (0,)), ((), ())), preferred_element_type=F32)


def _silu(x):
    return x * jax.nn.sigmoid(x)


def _rope(x, cos, sin, lo_half):
    swapped = jnp.where(lo_half, pltpu.roll(x, LANES - ATT_DIM // 2, 1), pltpu.roll(x, ATT_DIM // 2, 1))
    return x * cos + swapped * sin


def _hold_rows(b, period, offset):
    pieces = []
    for blk in range(CHUNK // period):
        row = blk * period + offset
        pieces.append(jnp.broadcast_to(b[row:row + 1, :], (period, LANES)))
    return pieces[0] if len(pieces) == 1 else jnp.concatenate(pieces, axis=0)


def _neg_abs(x):
    return -jnp.abs(x)


def _level_operands(q, k, b, lf, level, row_idx):
    half = 1 << level
    if level >= VREG_LEVEL:
        left, right = [], []
        for blk in range(CHUNK // (2 * half)):
            lo, mid, hi = blk * 2 * half, blk * 2 * half + half, (blk + 1) * 2 * half
            r = jnp.broadcast_to(b[mid - 1:mid, :], (half, LANES))
            left.append(k[lo:mid] * jnp.exp2(r - b[lo:mid]))
            right.append(q[mid:hi] * jnp.exp2(b[mid:hi] - r))
        both = [piece for pair in zip(left, right) for piece in pair]
        return jnp.concatenate(both, axis=0), (right[0] if len(right) == 1 else jnp.concatenate(right, axis=0))
    if half == 4:
        arg = _neg_abs(b - _hold_rows(b, 8, 3))
    elif half == 2:
        arg = _neg_abs(b - jnp.where((row_idx & 7) < 4, _hold_rows(b, 8, 1), _hold_rows(b, 8, 5)))
    else:
        arg = jnp.where((row_idx & 1) == 1, lf, 0.0)
    is_right = (row_idx & (2 * half - 1)) >= half
    return jnp.where(is_right, q, k) * jnp.exp2(arg), None


def _right_group_index(group, level):
    half_groups = (1 << level) // SUBLANES
    return (group // (2 * half_groups)) * half_groups + group % half_groups


def _layer_kernel(x_ref, win_ref, wout_ref, gpre_ref, gpost_ref, lbf_ref, oml_ref, lbd_ref, ghead_ref,
                  sink_ref, cosq_ref, sinq_ref, cosk_ref, sink_rot_ref, lvl_ref, tri_ref,
                  o_ref,
                  hn_s, qh_s, kh_s, vh_s, lf_s, b_s, zh_s, qa_s, kpad_s, vpad_s, za_s, yin_s, state_s,
                  sc_s, m_s, stbk_s, y_s, pst_s):
    T = SEQ_TILE
    j = pl.program_id(1)

    @pl.when(j == 0)
    def _():
        state_s[...] = jnp.zeros_like(state_s)
        kpad_s[:, 0:CHUNK, :] = jnp.zeros((2 * KV_HEADS, CHUNK, LANES), BF16)
        vpad_s[:, 0:CHUNK, :] = jnp.zeros((2 * KV_HEADS, CHUNK, LANES), BF16)

    lane = lax.broadcasted_iota(jnp.int32, (T, LANES), 1)
    lo_half = (lane & (ATT_DIM - 1)) < ATT_DIM // 2
    lo_head = lane < ATT_DIM
    row_c = lax.broadcasted_iota(jnp.int32, (CHUNK, LANES), 0)
    col_c = lax.broadcasted_iota(jnp.int32, (CHUNK, LANES), 1)
    causal = row_c >= col_c
    lvl = lvl_ref[...]
    ghead = ghead_ref[...]
    ones_blk = jnp.ones((2 * CHUNK, LANES), BF16)
    worst_spans = []

    def pre_norm():
        for rb in range(T // ROW_BLOCK):
            rows = slice(rb * ROW_BLOCK, (rb + 1) * ROW_BLOCK)
            x = x_ref[0, rows, :]
            ms = jnp.mean(x * x, axis=-1, keepdims=True)
            hn_s[rows, :] = (x * lax.rsqrt(ms + NORM_EPS) * gpre_ref[...]).astype(BF16)

    def proj(col):
        return _dot(hn_s[...], win_ref[:, col:col + COL_BLOCK])

    def proj_q(c):
        qf = _silu(proj(OFF_QH + c * COL_BLOCK))
        qh_s[2 * c] = qf[:, :LANES]
        qh_s[2 * c + 1] = qf[:, LANES:]

    def proj_f(c):
        cols = slice(c * COL_BLOCK, (c + 1) * COL_BLOCK)
        sig = jax.nn.sigmoid(proj(OFF_F + c * COL_BLOCK))
        oml = oml_ref[:, cols]
        lf = jnp.log2(lbf_ref[:, cols] + oml * sig)
        k = oml * (1.0 - sig) - lbd_ref[:, cols]
        lf_s[2 * c] = lf[:, :LANES]
        lf_s[2 * c + 1] = lf[:, LANES:]
        kh_s[2 * c] = k[:, :LANES]
        kh_s[2 * c + 1] = k[:, LANES:]

    def proj_i(c):
        res = proj(OFF_I + c * COL_BLOCK).astype(BF16)
        vh_s[2 * c] = res[:, :LANES]
        vh_s[2 * c + 1] = res[:, LANES:]

    def proj_z(c):
        res = proj(OFF_ZH + c * COL_BLOCK)
        zh_s[2 * c] = res[:, :LANES]
        zh_s[2 * c + 1] = res[:, LANES:]

    def proj_qa(c):
        res = proj(OFF_QA + c * COL_BLOCK)
        qa_s[2 * c] = _rope(res[:, :LANES], cosq_ref[...], sinq_ref[...], lo_half).astype(BF16)
        qa_s[2 * c + 1] = _rope(res[:, LANES:], cosq_ref[...], sinq_ref[...], lo_half).astype(BF16)

    def proj_kv():
        res = proj(OFF_KV)
        kr = _rope(res[:, :LANES], cosk_ref[...], sink_rot_ref[...], lo_half)
        kr_sw = pltpu.roll(kr, ATT_DIM, 1)
        va = res[:, LANES:]
        va_sw = pltpu.roll(va, ATT_DIM, 1)
        cur = slice(CHUNK, CHUNK + T)
        zero = jnp.zeros((T, LANES), F32)
        kpad_s[0, cur, :] = jnp.where(lo_head, kr, zero).astype(BF16)
        kpad_s[1, cur, :] = jnp.where(lo_head, zero, kr_sw).astype(BF16)
        kpad_s[2, cur, :] = jnp.where(lo_head, kr_sw, zero).astype(BF16)
        kpad_s[3, cur, :] = jnp.where(lo_head, zero, kr).astype(BF16)
        vpad_s[0, cur, :] = jnp.where(lo_head, va, zero).astype(BF16)
        vpad_s[1, cur, :] = jnp.where(lo_head, zero, va_sw).astype(BF16)
        vpad_s[2, cur, :] = jnp.where(lo_head, va_sw, zero).astype(BF16)
        vpad_s[3, cur, :] = jnp.where(lo_head, zero, va).astype(BF16)

    def proj_za(c):
        res = proj(OFF_ZA + c * COL_BLOCK)
        za_s[2 * c] = res[:, :LANES]
        za_s[2 * c + 1] = res[:, LANES:]

    def cumsum_pair(c):
        for ch in range(T // CHUNK):
            rows = slice(ch * CHUNK, (ch + 1) * CHUNK)
            lf = jnp.concatenate([lf_s[2 * c, rows, :], lf_s[2 * c + 1, rows, :]], axis=1)
            hi = lf.astype(BF16)
            lo = (lf - hi.astype(F32)).astype(BF16)
            b = _dot(tri_ref[...], hi) + _dot(tri_ref[...], lo)
            b_s[2 * c, rows, :] = b[:, :LANES]
            b_s[2 * c + 1, rows, :] = b[:, LANES:]
            for q0 in range(0, CHUNK, QUARTER):
                first = max(q0 - 1, 0)
                worst_spans.append(b[first:first + 1, :] - b[q0 + QUARTER - 1:q0 + QUARTER, :])

    def hgrn_load(h, ch):
        rows = pl.ds(ch * CHUNK, CHUNK)
        return rows, qh_s[h, rows, :], kh_s[h, rows, :], vh_s[h, rows, :], b_s[h, rows, :], state_s[h]

    def hgrn_finish(h, rows, q, k, v, b, st, scores, diag):
        b_last = b[CHUNK - 1:CHUNK, :]
        q_in = (q * jnp.exp2(b)).astype(BF16)
        o = _dot(scores.astype(BF16), v) + _dot_nt(q_in, st.astype(BF16))
        if diag is not None:
            o = o + diag * v.astype(F32)
        k_out = (k * jnp.exp2(b_last - b)).astype(BF16)
        state_s[h] = st * jnp.exp2(b_last) + _dot_tn(v, k_out)
        ms = jnp.mean(o * o, axis=-1, keepdims=True)
        on = o * lax.rsqrt(ms + NORM_EPS) * ghead
        yin_s[h, rows, :] = (on * _silu(zh_s[h, rows, :])).astype(BF16)

    def hgrn_head_robust(h, ch):
        rows, q, k, v, b, st = hgrn_load(h, ch)
        lf = lf_s[h, rows, :]
        scores = jnp.zeros((CHUNK, CHUNK), F32)
        for level in range(VREG_LEVEL):
            xk = _level_operands(q, k, b, lf, level, row_c)[0].astype(BF16)
            scores = jnp.where(lvl == level, _dot_nt(xk, xk), scores)
        prods = {}
        for level in range(VREG_LEVEL, LEVELS):
            xk, xr = _level_operands(q, k, b, lf, level, row_c)
            prods[level] = _dot_nt(xr.astype(BF16), xk.astype(BF16))
        groups = []
        for g in range(CHUNK // SUBLANES):
            rs = slice(g * SUBLANES, (g + 1) * SUBLANES)
            acc = scores[rs]
            for level in range(VREG_LEVEL, LEVELS):
                if (g * SUBLANES) & (1 << level):
                    ri = _right_group_index(g, level)
                    acc = jnp.where(lvl[rs] == level, prods[level][ri * SUBLANES:(ri + 1) * SUBLANES], acc)
            groups.append(acc)
        scores = jnp.concatenate(groups, axis=0)
        hgrn_finish(h, rows, q, k, v, b, st, scores, jnp.sum(q * k, axis=-1, keepdims=True))

    def hgrn_head_fast(h, ch):
        rows, q, k, v, b, st = hgrn_load(h, ch)
        mid = jnp.concatenate([jnp.broadcast_to(b[QUARTER - 1:QUARTER, :], (HALF, LANES)),
                               jnp.broadcast_to(b[HALF + QUARTER - 1:HALF + QUARTER, :], (HALF, LANES))], axis=0)
        d = b - mid
        q_own = (q * jnp.exp2(d)).astype(BF16)
        k_own = (k * jnp.exp2(-d)).astype(BF16)
        sep = jnp.broadcast_to(b[HALF - 1:HALF, :], (HALF, LANES))
        q_x = (q[HALF:] * jnp.exp2(b[HALF:] - sep)).astype(BF16)
        k_x = (k[:HALF] * jnp.exp2(sep - b[:HALF])).astype(BF16)
        none = jnp.zeros((HALF, LANES), BF16)
        top = _dot_nt(q_own[:HALF], k_own)
        lhs = jnp.concatenate([q_x, q_own[HALF:]], axis=1)
        rhs = jnp.concatenate([jnp.concatenate([k_x, none], axis=0),
                               jnp.concatenate([none, k_own[HALF:]], axis=0)], axis=1)
        scores = jnp.where(causal, jnp.concatenate([top, _dot_nt(lhs, rhs)], axis=0), 0.0)
        hgrn_finish(h, rows, q, k, v, b, st, scores, None)

    def attn_scores(n, g, pos):
        qrows = slice(n * CHUNK, (n + 1) * CHUNK)
        krows = slice(n * CHUNK, (n + 2) * CHUNK)
        pairs = range(g * PAIRS_PER_KV, (g + 1) * PAIRS_PER_KV)
        q_stack = jnp.concatenate([qa_s[p, qrows, :] for p in pairs], axis=0)
        s_all = _dot_nt(q_stack, kpad_s[2 * g + pos, krows, :])
        for i, p in enumerate(pairs):
            head = 2 * p + pos
            s = s_all[i * CHUNK:(i + 1) * CHUNK]
            s_prev = jnp.where(jnp.logical_or(j > 0, n > 0), s[:, :CHUNK], NEG_INF)
            sc = jnp.where(causal, s[:, CHUNK:], s_prev)
            sc_s[n % 2 * ATT_HEADS + head] = sc
            m = jnp.maximum(jnp.max(sc, axis=-1, keepdims=True), sink_ref[head])
            m_s[n % 2 * ATT_HEADS + head] = jnp.broadcast_to(m, (CHUNK, LANES))

    def attn_out(n, g):
        qrows = slice(n * CHUNK, (n + 1) * CHUNK)
        krows = slice(n * CHUNK, (n + 2) * CHUNK)
        pairs = range(g * PAIRS_PER_KV, (g + 1) * PAIRS_PER_KV)
        res = []
        for pos in range(2):
            for i, p in enumerate(pairs):
                head = 2 * p + pos
                pr = jnp.exp(sc_s[n % 2 * ATT_HEADS + head] - m_s[n % 2 * ATT_HEADS + head])
                pcat = jnp.concatenate([jnp.where(causal, 0.0, pr), jnp.where(causal, pr, 0.0)], axis=1)
                pst_s[pos, i * CHUNK:(i + 1) * CHUNK, :] = pcat.astype(BF16)
            vp = jnp.concatenate([vpad_s[2 * g + pos, krows, :], ones_blk], axis=1)
            res.append(_dot(pst_s[pos], vp))
        for i, p in enumerate(pairs):
            out = None
            for pos in range(2):
                head = 2 * p + pos
                r = res[pos][i * CHUNK:(i + 1) * CHUNK]
                denom = r[:, LANES:] + jnp.exp(sink_ref[head] - m_s[n % 2 * ATT_HEADS + head])
                o = r[:, :LANES] / denom
                out = o if out is None else out + o
            yin_s[HG_HEADS + p, qrows, :] = (out * _silu(za_s[p, qrows, :])).astype(BF16)

    def out_proj_hgrn(nb):
        cols = slice(nb * COL_BLOCK, (nb + 1) * COL_BLOCK)
        yin = jnp.concatenate([yin_s[i] for i in range(HG_HEADS)], axis=1)
        y_s[:, cols] = _dot(yin, wout_ref[0:HG_WIDTH, cols])

    def out_proj_attn(nb):
        cols = slice(nb * COL_BLOCK, (nb + 1) * COL_BLOCK)
        yin = jnp.concatenate([yin_s[HG_HEADS + i] for i in range(ATT_PAIRS)], axis=1)
        y_s[:, cols] = y_s[:, cols] + _dot(yin, wout_ref[HG_WIDTH:MIX_WIDTH, cols])

    def post_norm():
        for rb in range(T // ROW_BLOCK):
            rows = slice(rb * ROW_BLOCK, (rb + 1) * ROW_BLOCK)
            y = y_s[rows, :]
            ms = jnp.mean(y * y, axis=-1, keepdims=True)
            o_ref[0, rows, :] = x_ref[0, rows, :] + y * lax.rsqrt(ms + NORM_EPS) * gpost_ref[...]

    n_chunks = T // CHUNK
    n_pairs = HG_HEADS // 2
    assert ATT_WIDTH // COL_BLOCK == 2 * KV_HEADS and (n_chunks * KV_HEADS) % (D_MODEL // COL_BLOCK) == 0
    pre_norm()
    stbk_s[...] = state_s[...]
    for c in range(n_pairs + 1):
        if c < n_pairs:
            proj_steps = [lambda c=c: proj_q(c), lambda c=c: proj_f(c), lambda c=c: proj_i(c), lambda c=c: proj_z(c)]
        else:
            proj_steps = [proj_kv] + [lambda a=a: proj_qa(a) for a in range(ATT_WIDTH // COL_BLOCK)]
        mixer_steps = []
        if c > 0:
            mixer_steps = [lambda h=h, ch=ch: hgrn_head_fast(h, ch)
                           for ch in range(n_chunks) for h in (2 * c - 2, 2 * c - 1)]
        for i in range(max(len(proj_steps), len(mixer_steps))):
            if i < len(proj_steps):
                proj_steps[i]()
            if i < len(mixer_steps):
                mixer_steps[i]()
        if c < n_pairs:
            cumsum_pair(c)
    attn_units = [(g, pos) for g in range(KV_HEADS) for pos in range(2)]
    for a in range(ATT_WIDTH // COL_BLOCK):
        proj_za(a)
        attn_scores(0, *attn_units[a])
    out_every = n_chunks * KV_HEADS // (D_MODEL // COL_BLOCK)
    for n in range(n_chunks):
        for g in range(KV_HEADS):
            attn_out(n, g)
            if n + 1 < n_chunks:
                attn_scores(n + 1, g, 0)
                attn_scores(n + 1, g, 1)
            item = n * KV_HEADS + g
            if item % out_every == out_every - 1:
                out_proj_hgrn(item // out_every)

    worst = worst_spans[0]
    for span in worst_spans[1:]:
        worst = jnp.maximum(worst, span)
    fast_ok = jnp.max(worst) <= FAST_SPAN_LIMIT

    @pl.when(jnp.logical_not(fast_ok))
    def _():
        state_s[...] = stbk_s[...]
        for ch in range(n_chunks):
            def head_body(h, carry, ch=ch):
                hgrn_head_robust(h, ch)
                return carry
            lax.fori_loop(0, HG_HEADS, head_body, 0, unroll=ROBUST_HEAD_UNROLL)
        for nb in range(D_MODEL // COL_BLOCK):
            out_proj_hgrn(nb)

    kpad_s[:, 0:CHUNK, :] = kpad_s[:, T:T + CHUNK, :]
    vpad_s[:, 0:CHUNK, :] = vpad_s[:, T:T + CHUNK, :]

    for nb in range(D_MODEL // COL_BLOCK):
        out_proj_attn(nb)
    post_norm()


def _level_matrix():
    t = np.arange(CHUNK)[:, None]
    s = np.arange(CHUNK)[None, :]
    x = np.bitwise_xor(t, s)
    msb = np.where(x > 0, np.floor(np.log2(np.maximum(x, 1))), -1).astype(np.int32)
    return np.where(s < t, msb, -1).astype(np.int32)


def _rope_tables(seq):
    half = ATT_DIM // 2
    inv_freq = ROPE_THETA ** (-jnp.arange(half, dtype=F32) / half)
    ang = jnp.arange(seq, dtype=F32)[:, None] * inv_freq[None, :]
    cos = jnp.tile(jnp.cos(ang), (1, LANES // half))
    sin = jnp.sin(ang)
    sin = jnp.tile(jnp.concatenate([-sin, sin], axis=1), (1, LANES // ATT_DIM))
    return cos, sin


def _layer(x, w_in, w_out, g_pre, g_post, lb, g_head, sinks, tables):
    B, S, _ = x.shape
    T = SEQ_TILE
    cosq, sinq, cosk, sink_rot, lvl, tri = tables
    lbf = jnp.maximum(lb, LB_FLOOR)

    def const(shape):
        return pl.BlockSpec(shape, lambda b, j: (0,) * len(shape), pipeline_mode=pl.Buffered(1))

    def seq_table():
        return pl.BlockSpec((T, LANES), lambda b, j: (j, 0))

    in_specs = [
        pl.BlockSpec((1, T, D_MODEL), lambda b, j: (b, j, 0)),
        const((D_MODEL, IN_WIDTH)),
        const((MIX_WIDTH, D_MODEL)),
        const((1, D_MODEL)),
        const((1, D_MODEL)),
        const((1, HG_WIDTH)),
        const((1, HG_WIDTH)),
        const((1, HG_WIDTH)),
        const((1, HG_DIM)),
        pl.BlockSpec(memory_space=pltpu.SMEM),
        seq_table(), seq_table(), seq_table(), seq_table(),
        const((CHUNK, CHUNK)),
        const((CHUNK, CHUNK)),
    ]
    scratch = [
        pltpu.VMEM((T, D_MODEL), BF16),
        pltpu.VMEM((HG_HEADS, T, LANES), F32),
        pltpu.VMEM((HG_HEADS, T, LANES), F32),
        pltpu.VMEM((HG_HEADS, T, LANES), BF16),
        pltpu.VMEM((HG_HEADS, T, LANES), F32),
        pltpu.VMEM((HG_HEADS, T, LANES), F32),
        pltpu.VMEM((HG_HEADS, T, LANES), F32),
        pltpu.VMEM((ATT_PAIRS, T, LANES), BF16),
        pltpu.VMEM((2 * KV_HEADS, CHUNK + T, LANES), BF16),
        pltpu.VMEM((2 * KV_HEADS, CHUNK + T, LANES), BF16),
        pltpu.VMEM((ATT_PAIRS, T, LANES), F32),
        pltpu.VMEM((MIX_WIDTH // LANES, T, LANES), BF16),
        pltpu.VMEM((HG_HEADS, HG_DIM, HG_DIM), F32),
        pltpu.VMEM((2 * ATT_HEADS, CHUNK, LANES), F32),
        pltpu.VMEM((2 * ATT_HEADS, CHUNK, LANES), F32),
        pltpu.VMEM((HG_HEADS, HG_DIM, HG_DIM), F32),
        pltpu.VMEM((T, D_MODEL), F32),
        pltpu.VMEM((2, PAIRS_PER_KV * CHUNK, 2 * CHUNK), BF16),
    ]
    return pl.pallas_call(
        _layer_kernel,
        out_shape=jax.ShapeDtypeStruct(x.shape, x.dtype),
        grid=(B, S // T),
        in_specs=in_specs,
        out_specs=pl.BlockSpec((1, T, D_MODEL), lambda b, j: (b, j, 0)),
        scratch_shapes=scratch,
        compiler_params=pltpu.CompilerParams(
            dimension_semantics=("arbitrary", "arbitrary"),
            vmem_limit_bytes=VMEM_LIMIT_BYTES),
        name="hybrid_layer",
    )(x, w_in.astype(BF16), w_out.astype(BF16), g_pre[None, :], g_post[None, :],
      lbf[None, :], (1.0 - lb)[None, :], (lbf - lb)[None, :], g_head[None, :], sinks,
      cosq, sinq, cosk, sink_rot, lvl, tri)


def kernel(x, w_in, w_out, g_pre, g_post, lb_param, g_head, sinks):
    depth = w_in.shape[0]
    seq = x.shape[1]
    p = jax.nn.softmax(lb_param.astype(F32), axis=0)
    lower_bounds = jnp.cumsum(p, axis=0) - p[0:1]
    cos, sin = _rope_tables(seq)
    tri = jnp.asarray(np.tril(np.ones((CHUNK, CHUNK), np.float32)), BF16)
    tables = (cos * ATT_SCALE, sin * ATT_SCALE, cos, sin, jnp.asarray(_level_matrix()), tri)
    for l in range(depth):
        x = _layer(x, w_in[l], w_out[l], g_pre[l], g_post[l], lower_bounds[l], g_head[l], sinks[l], tables)
    return x
```

```python
import math

import numpy as np
import jax
import jax.numpy as jnp
from jax import lax
from jax.experimental import pallas as pl
from jax.experimental.pallas import tpu as pltpu

D_MODEL = 1024
HG_WIDTH = 1024
HG_HEADS = 8
HG_DIM = 128
ATT_WIDTH = 1024
ATT_HEADS = 16
ATT_DIM = 64
ATT_PAIRS = ATT_HEADS // 2
KV_HEADS = 2
KV_WIDTH = 128
PAIRS_PER_KV = ATT_PAIRS // KV_HEADS
WINDOW = 128
IN_WIDTH = 6400
MIX_WIDTH = 2048
ATT_SCALE = 1.0 / math.sqrt(ATT_DIM)
ROPE_THETA = 10000.0
NORM_EPS = 1e-6
NEG_INF = -1e30
LB_FLOOR = 1e-20

LANES = 128
SUBLANES = 8
CHUNK = 128
HALF = CHUNK // 2
QUARTER = CHUNK // 4
FAST_SPAN_LIMIT = 64.0
LEVELS = 7
VREG_LEVEL = 3
SEQ_TILE = 256
COL_BLOCK = 256
ROW_BLOCK = 64
ROBUST_HEAD_UNROLL = 4
VMEM_LIMIT_BYTES = 56 * 1024 * 1024

OFF_QH, OFF_F, OFF_I, OFF_ZH, OFF_QA, OFF_KV, OFF_ZA = 0, 1024, 2048, 3072, 4096, 5120, 5376

F32 = jnp.float32
BF16 = jnp.bfloat16


def _dot(a, b):
    return jnp.dot(a, b, preferred_element_type=F32)


def _dot_nt(a, b):
    return lax.dot_general(a, b, (((1,), (1,)), ((), ())), preferred_element_type=F32)


def _dot_tn(a, b):
    return lax.dot_general(a, b, (((0,), (0,)), ((), ())), preferred_element_type=F32)


def _silu(x):
    return x * jax.nn.sigmoid(x)


def _rope(x, cos, sin, lo_half):
    swapped = jnp.where(lo_half, pltpu.roll(x, LANES - ATT_DIM // 2, 1), pltpu.roll(x, ATT_DIM // 2, 1))
    return x * cos + swapped * sin


def _hold_rows(b, period, offset):
    pieces = []
    for blk in range(CHUNK // period):
        row = blk * period + offset
        pieces.append(jnp.broadcast_to(b[row:row + 1, :], (period, LANES)))
    return pieces[0] if len(pieces) == 1 else jnp.concatenate(pieces, axis=0)


def _neg_abs(x):
    return -jnp.abs(x)


def _level_operands(q, k, b, lf, level, row_idx):
    half = 1 << level
    if level >= VREG_LEVEL:
        left, right = [], []
        for blk in range(CHUNK // (2 * half)):
            lo, mid, hi = blk * 2 * half, blk * 2 * half + half, (blk + 1) * 2 * half
            r = jnp.broadcast_to(b[mid - 1:mid, :], (half, LANES))
            left.append(k[lo:mid] * jnp.exp2(r - b[lo:mid]))
            right.append(q[mid:hi] * jnp.exp2(b[mid:hi] - r))
        both = [piece for pair in zip(left, right) for piece in pair]
        return jnp.concatenate(both, axis=0), (right[0] if len(right) == 1 else jnp.concatenate(right, axis=0))
    if half == 4:
        arg = _neg_abs(b - _hold_rows(b, 8, 3))
    elif half == 2:
        arg = _neg_abs(b - jnp.where((row_idx & 7) < 4, _hold_rows(b, 8, 1), _hold_rows(b, 8, 5)))
    else:
        arg = jnp.where((row_idx & 1) == 1, lf, 0.0)
    is_right = (row_idx & (2 * half - 1)) >= half
    return jnp.where(is_right, q, k) * jnp.exp2(arg), None


def _right_group_index(group, level):
    half_groups = (1 << level) // SUBLANES
    return (group // (2 * half_groups)) * half_groups + group % half_groups


def _layer_kernel(x_ref, win_ref, wout_ref, gpre_ref, gpost_ref, lbf_ref, oml_ref, lbd_ref, ghead_ref,
                  sink_ref, cosq_ref, sinq_ref, cosk_ref, sink_rot_ref, lvl_ref, tri_ref,
                  o_ref,
                  hn_s, qh_s, kh_s, vh_s, lf_s, b_s, zh_s, qa_s, kpad_s, vpad_s, za_s, yin_s, state_s,
                  sc_s, m_s, stbk_s, y_s, pst_s):
    T = SEQ_TILE
    j = pl.program_id(1)

    @pl.when(j == 0)
    def _():
        state_s[...] = jnp.zeros_like(state_s)
        kpad_s[:, 0:CHUNK, :] = jnp.zeros((2 * KV_HEADS, CHUNK, LANES), BF16)
        vpad_s[:, 0:CHUNK, :] = jnp.zeros((2 * KV_HEADS, CHUNK, LANES), BF16)

    lane = lax.broadcasted_iota(jnp.int32, (T, LANES), 1)
    lo_half = (lane & (ATT_DIM - 1)) < ATT_DIM // 2
    lo_head = lane < ATT_DIM
    row_c = lax.broadcasted_iota(jnp.int32, (CHUNK, LANES), 0)
    col_c = lax.broadcasted_iota(jnp.int32, (CHUNK, LANES), 1)
    causal = row_c >= col_c
    lvl = lvl_ref[...]
    ghead = ghead_ref[...]
    ones_blk = jnp.ones((2 * CHUNK, LANES), BF16)
    worst_spans = []

    def pre_norm():
        for rb in range(T // ROW_BLOCK):
            rows = slice(rb * ROW_BLOCK, (rb + 1) * ROW_BLOCK)
            x = x_ref[0, rows, :]
            ms = jnp.mean(x * x, axis=-1, keepdims=True)
            hn_s[rows, :] = (x * lax.rsqrt(ms + NORM_EPS) * gpre_ref[...]).astype(BF16)

    def proj(col):
        return _dot(hn_s[...], win_ref[:, col:col + COL_BLOCK])

    def proj_q(c):
        qf = _silu(proj(OFF_QH + c * COL_BLOCK))
        qh_s[2 * c] = qf[:, :LANES]
        qh_s[2 * c + 1] = qf[:, LANES:]

    def proj_f(c):
        cols = slice(c * COL_BLOCK, (c + 1) * COL_BLOCK)
        sig = jax.nn.sigmoid(proj(OFF_F + c * COL_BLOCK))
        oml = oml_ref[:, cols]
        lf = jnp.log2(lbf_ref[:, cols] + oml * sig)
        k = oml * (1.0 - sig) - lbd_ref[:, cols]
        lf_s[2 * c] = lf[:, :LANES]
        lf_s[2 * c + 1] = lf[:, LANES:]
        kh_s[2 * c] = k[:, :LANES]
        kh_s[2 * c + 1] = k[:, LANES:]

    def proj_i(c):
        res = proj(OFF_I + c * COL_BLOCK).astype(BF16)
        vh_s[2 * c] = res[:, :LANES]
        vh_s[2 * c + 1] = res[:, LANES:]

    def proj_z(c):
        res = proj(OFF_ZH + c * COL_BLOCK)
        zh_s[2 * c] = res[:, :LANES]
        zh_s[2 * c + 1] = res[:, LANES:]

    def proj_qa(c):
        res = proj(OFF_QA + c * COL_BLOCK)
        qa_s[2 * c] = _rope(res[:, :LANES], cosq_ref[...], sinq_ref[...], lo_half).astype(BF16)
        qa_s[2 * c + 1] = _rope(res[:, LANES:], cosq_ref[...], sinq_ref[...], lo_half).astype(BF16)

    def proj_kv():
        res = proj(OFF_KV)
        kr = _rope(res[:, :LANES], cosk_ref[...], sink_rot_ref[...], lo_half)
        kr_sw = pltpu.roll(kr, ATT_DIM, 1)
        va = res[:, LANES:]
        va_sw = pltpu.roll(va, ATT_DIM, 1)
        cur = slice(CHUNK, CHUNK + T)
        zero = jnp.zeros((T, LANES), F32)
        kpad_s[0, cur, :] = jnp.where(lo_head, kr, zero).astype(BF16)
        kpad_s[1, cur, :] = jnp.where(lo_head, zero, kr_sw).astype(BF16)
        kpad_s[2, cur, :] = jnp.where(lo_head, kr_sw, zero).astype(BF16)
        kpad_s[3, cur, :] = jnp.where(lo_head, zero, kr).astype(BF16)
        vpad_s[0, cur, :] = jnp.where(lo_head, va, zero).astype(BF16)
        vpad_s[1, cur, :] = jnp.where(lo_head, zero, va_sw).astype(BF16)
        vpad_s[2, cur, :] = jnp.where(lo_head, va_sw, zero).astype(BF16)
        vpad_s[3, cur, :] = jnp.where(lo_head, zero, va).astype(BF16)

    def proj_za(c):
        res = proj(OFF_ZA + c * COL_BLOCK)
        za_s[2 * c] = res[:, :LANES]
        za_s[2 * c + 1] = res[:, LANES:]

    def cumsum_pair(c):
        for ch in range(T // CHUNK):
            rows = slice(ch * CHUNK, (ch + 1) * CHUNK)
            lf = jnp.concatenate([lf_s[2 * c, rows, :], lf_s[2 * c + 1, rows, :]], axis=1)
            hi = lf.astype(BF16)
            lo = (lf - hi.astype(F32)).astype(BF16)
            b = _dot(tri_ref[...], hi) + _dot(tri_ref[...], lo)
            b_s[2 * c, rows, :] = b[:, :LANES]
            b_s[2 * c + 1, rows, :] = b[:, LANES:]
            for q0 in range(0, CHUNK, QUARTER):
                first = max(q0 - 1, 0)
                worst_spans.append(b[first:first + 1, :] - b[q0 + QUARTER - 1:q0 + QUARTER, :])

    def hgrn_load(h, ch):
        rows = pl.ds(ch * CHUNK, CHUNK)
        return rows, qh_s[h, rows, :], kh_s[h, rows, :], vh_s[h, rows, :], b_s[h, rows, :], state_s[h]

    def hgrn_finish(h, rows, q, k, v, b, st, scores, diag):
        b_last = b[CHUNK - 1:CHUNK, :]
        q_in = (q * jnp.exp2(b)).astype(BF16)
        o = _dot(scores.astype(BF16), v) + _dot_nt(q_in, st.astype(BF16))
        if diag is not None:
            o = o + diag * v.astype(F32)
        k_out = (k * jnp.exp2(b_last - b)).astype(BF16)
        state_s[h] = st * jnp.exp2(b_last) + _dot_tn(v, k_out)
        ms = jnp.mean(o * o, axis=-1, keepdims=True)
        on = o * lax.rsqrt(ms + NORM_EPS) * ghead
        yin_s[h, rows, :] = (on * _silu(zh_s[h, rows, :])).astype(BF16)

    def hgrn_head_robust(h, ch):
        rows, q, k, v, b, st = hgrn_load(h, ch)
        lf = lf_s[h, rows, :]
        scores = jnp.zeros((CHUNK, CHUNK), F32)
        for level in range(VREG_LEVEL):
            xk = _level_operands(q, k, b, lf, level, row_c)[0].astype(BF16)
            scores = jnp.where(lvl == level, _dot_nt(xk, xk), scores)
        prods = {}
        for level in range(VREG_LEVEL, LEVELS):
            xk, xr = _level_operands(q, k, b, lf, level, row_c)
            prods[level] = _dot_nt(xr.astype(BF16), xk.astype(BF16))
        groups = []
        for g in range(CHUNK // SUBLANES):
            rs = slice(g * SUBLANES, (g + 1) * SUBLANES)
            acc = scores[rs]
            for level in range(VREG_LEVEL, LEVELS):
                if (g * SUBLANES) & (1 << level):
                    ri = _right_group_index(g, level)
                    acc = jnp.where(lvl[rs] == level, prods[level][ri * SUBLANES:(ri + 1) * SUBLANES], acc)
            groups.append(acc)
        scores = jnp.concatenate(groups, axis=0)
        hgrn_finish(h, rows, q, k, v, b, st, scores, jnp.sum(q * k, axis=-1, keepdims=True))

    def hgrn_head_fast(h, ch):
        rows, q, k, v, b, st = hgrn_load(h, ch)
        mid_lo = jnp.broadcast_to(b[QUARTER - 1:QUARTER, :], (HALF, LANES))
        mid_hi = jnp.broadcast_to(b[HALF + QUARTER - 1:HALF + QUARTER, :], (HALF, LANES))
        d = b - jnp.concatenate([mid_lo, mid_hi], axis=0)
        q_own = (q * jnp.exp2(d)).astype(BF16)
        k_own = (k * jnp.exp2(-d)).astype(BF16)
        q_cross = (q[HALF:] * jnp.exp2(b[HALF:] - mid_lo)).astype(BF16)
        res = _dot_nt(jnp.concatenate([q_own, q_cross], axis=0), k_own)
        first_half_keys = lax.broadcasted_iota(jnp.int32, (HALF, LANES), 1) < HALF
        bottom = jnp.where(first_half_keys, res[CHUNK:], res[HALF:CHUNK])
        scores = jnp.where(causal, jnp.concatenate([res[:HALF], bottom], axis=0), 0.0)
        hgrn_finish(h, rows, q, k, v, b, st, scores, None)

    def attn_scores(n, g, pos):
        qrows = slice(n * CHUNK, (n + 1) * CHUNK)
        krows = slice(n * CHUNK, (n + 2) * CHUNK)
        pairs = range(g * PAIRS_PER_KV, (g + 1) * PAIRS_PER_KV)
        q_stack = jnp.concatenate([qa_s[p, qrows, :] for p in pairs], axis=0)
        s_all = _dot_nt(q_stack, kpad_s[2 * g + pos, krows, :])
        for i, p in enumerate(pairs):
            head = 2 * p + pos
            s = s_all[i * CHUNK:(i + 1) * CHUNK]
            s_prev = jnp.where(jnp.logical_or(j > 0, n > 0), s[:, :CHUNK], NEG_INF)
            sc = jnp.where(causal, s[:, CHUNK:], s_prev)
            sc_s[n % 2 * ATT_HEADS + head] = sc
            m = jnp.maximum(jnp.max(sc, axis=-1, keepdims=True), sink_ref[head])
            m_s[n % 2 * ATT_HEADS + head] = jnp.broadcast_to(m, (CHUNK, LANES))

    def attn_out(n, g):
        qrows = slice(n * CHUNK, (n + 1) * CHUNK)
        krows = slice(n * CHUNK, (n + 2) * CHUNK)
        pairs = range(g * PAIRS_PER_KV, (g + 1) * PAIRS_PER_KV)
        res = []
        for pos in range(2):
            for i, p in enumerate(pairs):
                head = 2 * p + pos
                pr = jnp.exp(sc_s[n % 2 * ATT_HEADS + head] - m_s[n % 2 * ATT_HEADS + head])
                pcat = jnp.concatenate([jnp.where(causal, 0.0, pr), jnp.where(causal, pr, 0.0)], axis=1)
                pst_s[pos, i * CHUNK:(i + 1) * CHUNK, :] = pcat.astype(BF16)
            vp = jnp.concatenate([vpad_s[2 * g + pos, krows, :], ones_blk], axis=1)
            res.append(_dot(pst_s[pos], vp))
        for i, p in enumerate(pairs):
            out = None
            for pos in range(2):
                head = 2 * p + pos
                r = res[pos][i * CHUNK:(i + 1) * CHUNK]
                denom = r[:, LANES:] + jnp.exp(sink_ref[head] - m_s[n % 2 * ATT_HEADS + head])
                o = r[:, :LANES] / denom
                out = o if out is None else out + o
            yin_s[HG_HEADS + p, qrows, :] = (out * _silu(za_s[p, qrows, :])).astype(BF16)

    def out_proj_hgrn(nb):
        cols = slice(nb * COL_BLOCK, (nb + 1) * COL_BLOCK)
        yin = jnp.concatenate([yin_s[i] for i in range(HG_HEADS)], axis=1)
        y_s[:, cols] = _dot(yin, wout_ref[0:HG_WIDTH, cols])

    def out_proj_attn(nb):
        cols = slice(nb * COL_BLOCK, (nb + 1) * COL_BLOCK)
        yin = jnp.concatenate([yin_s[HG_HEADS + i] for i in range(ATT_PAIRS)], axis=1)
        y_s[:, cols] = y_s[:, cols] + _dot(yin, wout_ref[HG_WIDTH:MIX_WIDTH, cols])

    def post_norm():
        for rb in range(T // ROW_BLOCK):
            rows = slice(rb * ROW_BLOCK, (rb + 1) * ROW_BLOCK)
            y = y_s[rows, :]
            ms = jnp.mean(y * y, axis=-1, keepdims=True)
            o_ref[0, rows, :] = x_ref[0, rows, :] + y * lax.rsqrt(ms + NORM_EPS) * gpost_ref[...]

    n_chunks = T // CHUNK
    n_pairs = HG_HEADS // 2
    assert ATT_WIDTH // COL_BLOCK == 2 * KV_HEADS and (n_chunks * KV_HEADS) % (D_MODEL // COL_BLOCK) == 0
    pre_norm()
    stbk_s[...] = state_s[...]
    for c in range(n_pairs + 1):
        if c < n_pairs:
            proj_steps = [lambda c=c: proj_q(c), lambda c=c: proj_f(c), lambda c=c: proj_i(c), lambda c=c: proj_z(c)]
        else:
            proj_steps = [proj_kv] + [lambda a=a: proj_qa(a) for a in range(ATT_WIDTH // COL_BLOCK)]
        mixer_steps = []
        if c > 0:
            mixer_steps = [lambda h=h, ch=ch: hgrn_head_fast(h, ch)
                           for ch in range(n_chunks) for h in (2 * c - 2, 2 * c - 1)]
        for i in range(max(len(proj_steps), len(mixer_steps))):
            if i < len(proj_steps):
                proj_steps[i]()
            if i < len(mixer_steps):
                mixer_steps[i]()
        if c < n_pairs:
            cumsum_pair(c)
    attn_units = [(g, pos) for g in range(KV_HEADS) for pos in range(2)]
    for a in range(ATT_WIDTH // COL_BLOCK):
        proj_za(a)
        attn_scores(0, *attn_units[a])
    out_every = n_chunks * KV_HEADS // (D_MODEL // COL_BLOCK)
    for n in range(n_chunks):
        for g in range(KV_HEADS):
            attn_out(n, g)
            if n + 1 < n_chunks:
                attn_scores(n + 1, g, 0)
                attn_scores(n + 1, g, 1)
            item = n * KV_HEADS + g
            if item % out_every == out_every - 1:
                out_proj_hgrn(item // out_every)

    worst = worst_spans[0]
    for span in worst_spans[1:]:
        worst = jnp.maximum(worst, span)
    fast_ok = jnp.max(worst) <= FAST_SPAN_LIMIT

    @pl.when(jnp.logical_not(fast_ok))
    def _():
        state_s[...] = stbk_s[...]
        for ch in range(n_chunks):
            def head_body(h, carry, ch=ch):
                hgrn_head_robust(h, ch)
                return carry
            lax.fori_loop(0, HG_HEADS, head_body, 0, unroll=ROBUST_HEAD_UNROLL)
        for nb in range(D_MODEL // COL_BLOCK):
            out_proj_hgrn(nb)

    kpad_s[:, 0:CHUNK, :] = kpad_s[:, T:T + CHUNK, :]
    vpad_s[:, 0:CHUNK, :] = vpad_s[:, T:T + CHUNK, :]

    for nb in range(D_MODEL // COL_BLOCK):
        out_proj_attn(nb)
    post_norm()


def _level_matrix():
    t = np.arange(CHUNK)[:, None]
    s = np.arange(CHUNK)[None, :]
    x = np.bitwise_xor(t, s)
    msb = np.where(x > 0, np.floor(np.log2(np.maximum(x, 1))), -1).astype(np.int32)
    return np.where(s < t, msb, -1).astype(np.int32)


def _rope_tables(seq):
    half = ATT_DIM // 2
    inv_freq = ROPE_THETA ** (-jnp.arange(half, dtype=F32) / half)
    ang = jnp.arange(seq, dtype=F32)[:, None] * inv_freq[None, :]
    cos = jnp.tile(jnp.cos(ang), (1, LANES // half))
    sin = jnp.sin(ang)
    sin = jnp.tile(jnp.concatenate([-sin, sin], axis=1), (1, LANES // ATT_DIM))
    return cos, sin


def _layer(x, w_in, w_out, g_pre, g_post, lb, g_head, sinks, tables):
    B, S, _ = x.shape
    T = SEQ_TILE
    cosq, sinq, cosk, sink_rot, lvl, tri = tables
    lbf = jnp.maximum(lb, LB_FLOOR)

    def const(shape):
        return pl.BlockSpec(shape, lambda b, j: (0,) * len(shape), pipeline_mode=pl.Buffered(1))

    def seq_table():
        return pl.BlockSpec((T, LANES), lambda b, j: (j, 0))

    in_specs = [
        pl.BlockSpec((1, T, D_MODEL), lambda b, j: (b, j, 0)),
        const((D_MODEL, IN_WIDTH)),
        const((MIX_WIDTH, D_MODEL)),
        const((1, D_MODEL)),
        const((1, D_MODEL)),
        const((1, HG_WIDTH)),
        const((1, HG_WIDTH)),
        const((1, HG_WIDTH)),
        const((1, HG_DIM)),
        pl.BlockSpec(memory_space=pltpu.SMEM),
        seq_table(), seq_table(), seq_table(), seq_table(),
        const((CHUNK, CHUNK)),
        const((CHUNK, CHUNK)),
    ]
    scratch = [
        pltpu.VMEM((T, D_MODEL), BF16),
        pltpu.VMEM((HG_HEADS, T, LANES), F32),
        pltpu.VMEM((HG_HEADS, T, LANES), F32),
        pltpu.VMEM((HG_HEADS, T, LANES), BF16),
        pltpu.VMEM((HG_HEADS, T, LANES), F32),
        pltpu.VMEM((HG_HEADS, T, LANES), F32),
        pltpu.VMEM((HG_HEADS, T, LANES), F32),
        pltpu.VMEM((ATT_PAIRS, T, LANES), BF16),
        pltpu.VMEM((2 * KV_HEADS, CHUNK + T, LANES), BF16),
        pltpu.VMEM((2 * KV_HEADS, CHUNK + T, LANES), BF16),
        pltpu.VMEM((ATT_PAIRS, T, LANES), F32),
        pltpu.VMEM((MIX_WIDTH // LANES, T, LANES), BF16),
        pltpu.VMEM((HG_HEADS, HG_DIM, HG_DIM), F32),
        pltpu.VMEM((2 * ATT_HEADS, CHUNK, LANES), F32),
        pltpu.VMEM((2 * ATT_HEADS, CHUNK, LANES), F32),
        pltpu.VMEM((HG_HEADS, HG_DIM, HG_DIM), F32),
        pltpu.VMEM((T, D_MODEL), F32),
        pltpu.VMEM((2, PAIRS_PER_KV * CHUNK, 2 * CHUNK), BF16),
    ]
    return pl.pallas_call(
        _layer_kernel,
        out_shape=jax.ShapeDtypeStruct(x.shape, x.dtype),
        grid=(B, S // T),
        in_specs=in_specs,
        out_specs=pl.BlockSpec((1, T, D_MODEL), lambda b, j: (b, j, 0)),
        scratch_shapes=scratch,
        compiler_params=pltpu.CompilerParams(
            dimension_semantics=("arbitrary", "arbitrary"),
            vmem_limit_bytes=VMEM_LIMIT_BYTES),
        name="hybrid_layer",
    )(x, w_in.astype(BF16), w_out.astype(BF16), g_pre[None, :], g_post[None, :],
      lbf[None, :], (1.0 - lb)[None, :], (lbf - lb)[None, :], g_head[None, :], sinks,
      cosq, sinq, cosk, sink_rot, lvl, tri)


def kernel(x, w_in, w_out, g_pre, g_post, lb_param, g_head, sinks):
    depth = w_in.shape[0]
    seq = x.shape[1]
    p = jax.nn.softmax(lb_param.astype(F32), axis=0)
    lower_bounds = jnp.cumsum(p, axis=0) - p[0:1]
    cos, sin = _rope_tables(seq)
    tri = jnp.asarray(np.tril(np.ones((CHUNK, CHUNK), np.float32)), BF16)
    tables = (cos * ATT_SCALE, sin * ATT_SCALE, cos, sin, jnp.asarray(_level_matrix()), tri)
    for l in range(depth):
        x = _layer(x, w_in[l], w_out[l], g_pre[l], g_post[l], lower_bounds[l], g_head[l], sinks[l], tables)
    return x
```

```python
import math

import numpy as np
import jax
import jax.numpy as jnp
from jax import lax
from jax.experimental import pallas as pl
from jax.experimental.pallas import tpu as pltpu

D_MODEL = 1024
HG_WIDTH = 1024
HG_HEADS = 8
HG_DIM = 128
ATT_WIDTH = 1024
ATT_HEADS = 16
ATT_DIM = 64
ATT_PAIRS = ATT_HEADS // 2
KV_HEADS = 2
KV_WIDTH = 128
PAIRS_PER_KV = ATT_PAIRS // KV_HEADS
WINDOW = 128
IN_WIDTH = 6400
MIX_WIDTH = 2048
ATT_SCALE = 1.0 / math.sqrt(ATT_DIM)
ROPE_THETA = 10000.0
NORM_EPS = 1e-6
NEG_INF = -1e30
LB_FLOOR = 1e-20

LANES = 128
SUBLANES = 8
CHUNK = 128
HALF = CHUNK // 2
QUARTER = CHUNK // 4
FAST_SPAN_LIMIT = 64.0
LEVELS = 7
VREG_LEVEL = 3
SEQ_TILE = 256
COL_BLOCK = 256
ROW_BLOCK = 64
ROBUST_HEAD_UNROLL = 4
VMEM_LIMIT_BYTES = 56 * 1024 * 1024

OFF_QH, OFF_F, OFF_I, OFF_ZH, OFF_QA, OFF_KV, OFF_ZA = 0, 1024, 2048, 3072, 4096, 5120, 5376

F32 = jnp.float32
BF16 = jnp.bfloat16


def _dot(a, b):
    return jnp.dot(a, b, preferred_element_type=F32)


def _dot_nt(a, b):
    return lax.dot_general(a, b, (((1,), (1,)), ((), ())), preferred_element_type=F32)


def _dot_tn(a, b):
    return lax.dot_general(a, b, (((0,), (0,)), ((), ())), preferred_element_type=F32)


def _silu(x):
    return x * jax.nn.sigmoid(x)


def _rope(x, cos, sin, lo_half):
    swapped = jnp.where(lo_half, pltpu.roll(x, LANES - ATT_DIM // 2, 1), pltpu.roll(x, ATT_DIM // 2, 1))
    return x * cos + swapped * sin


def _hold_rows(b, period, offset):
    pieces = []
    for blk in range(CHUNK // period):
        row = blk * period + offset
        pieces.append(jnp.broadcast_to(b[row:row + 1, :], (period, LANES)))
    return pieces[0] if len(pieces) == 1 else jnp.concatenate(pieces, axis=0)


def _neg_abs(x):
    return -jnp.abs(x)


def _level_operands(q, k, b, lf, level, row_idx):
    half = 1 << level
    if level >= VREG_LEVEL:
        left, right = [], []
        for blk in range(CHUNK // (2 * half)):
            lo, mid, hi = blk * 2 * half, blk * 2 * half + half, (blk + 1) * 2 * half
            r = jnp.broadcast_to(b[mid - 1:mid, :], (half, LANES))
            left.append(k[lo:mid] * jnp.exp2(r - b[lo:mid]))
            right.append(q[mid:hi] * jnp.exp2(b[mid:hi] - r))
        both = [piece for pair in zip(left, right) for piece in pair]
        return jnp.concatenate(both, axis=0), (right[0] if len(right) == 1 else jnp.concatenate(right, axis=0))
    if half == 4:
        arg = _neg_abs(b - _hold_rows(b, 8, 3))
    elif half == 2:
        arg = _neg_abs(b - jnp.where((row_idx & 7) < 4, _hold_rows(b, 8, 1), _hold_rows(b, 8, 5)))
    else:
        arg = jnp.where((row_idx & 1) == 1, lf, 0.0)
    is_right = (row_idx & (2 * half - 1)) >= half
    return jnp.where(is_right, q, k) * jnp.exp2(arg), None


def _right_group_index(group, level):
    half_groups = (1 << level) // SUBLANES
    return (group // (2 * half_groups)) * half_groups + group % half_groups


def _layer_kernel(x_ref, win_ref, wout_ref, gpre_ref, gpost_ref, lbf_ref, oml_ref, lbd_ref, ghead_ref,
                  sink_ref, cosq_ref, sinq_ref, cosk_ref, sink_rot_ref, lvl_ref, tri_ref,
                  o_ref,
                  hn_s, qh_s, kh_s, vh_s, lf_s, b_s, zh_s, qa_s, kpad_s, vpad_s, za_s, yin_s, state_s,
                  sc_s, m_s, stbk_s, y_s, pst_s):
    T = SEQ_TILE
    j = pl.program_id(1)

    @pl.when(j == 0)
    def _():
        state_s[...] = jnp.zeros_like(state_s)
        kpad_s[:, 0:CHUNK, :] = jnp.zeros((2 * KV_HEADS, CHUNK, LANES), BF16)
        vpad_s[:, 0:CHUNK, :] = jnp.zeros((2 * KV_HEADS, CHUNK, LANES), BF16)

    lane = lax.broadcasted_iota(jnp.int32, (T, LANES), 1)
    lo_half = (lane & (ATT_DIM - 1)) < ATT_DIM // 2
    lo_head = lane < ATT_DIM
    row_c = lax.broadcasted_iota(jnp.int32, (CHUNK, LANES), 0)
    col_c = lax.broadcasted_iota(jnp.int32, (CHUNK, LANES), 1)
    causal = row_c >= col_c
    lvl = lvl_ref[...]
    ghead = ghead_ref[...]
    ones_blk = jnp.ones((2 * CHUNK, LANES), BF16)
    worst_spans = []

    def pre_norm():
        for rb in range(T // ROW_BLOCK):
            rows = slice(rb * ROW_BLOCK, (rb + 1) * ROW_BLOCK)
            x = x_ref[0, rows, :]
            ms = jnp.mean(x * x, axis=-1, keepdims=True)
            hn_s[rows, :] = (x * lax.rsqrt(ms + NORM_EPS) * gpre_ref[...]).astype(BF16)

    def proj(col):
        return _dot(hn_s[...], win_ref[:, col:col + COL_BLOCK])

    def proj_q(c):
        qf = _silu(proj(OFF_QH + c * COL_BLOCK))
        qh_s[2 * c] = qf[:, :LANES]
        qh_s[2 * c + 1] = qf[:, LANES:]

    def proj_f(c):
        cols = slice(c * COL_BLOCK, (c + 1) * COL_BLOCK)
        sig = jax.nn.sigmoid(proj(OFF_F + c * COL_BLOCK))
        oml = oml_ref[:, cols]
        lf = jnp.log2(lbf_ref[:, cols] + oml * sig)
        k = oml * (1.0 - sig) - lbd_ref[:, cols]
        lf_s[2 * c] = lf[:, :LANES]
        lf_s[2 * c + 1] = lf[:, LANES:]
        kh_s[2 * c] = k[:, :LANES]
        kh_s[2 * c + 1] = k[:, LANES:]

    def proj_i(c):
        res = proj(OFF_I + c * COL_BLOCK).astype(BF16)
        vh_s[2 * c] = res[:, :LANES]
        vh_s[2 * c + 1] = res[:, LANES:]

    def proj_z(c):
        res = proj(OFF_ZH + c * COL_BLOCK)
        zh_s[2 * c] = res[:, :LANES]
        zh_s[2 * c + 1] = res[:, LANES:]

    def proj_qa(c):
        res = proj(OFF_QA + c * COL_BLOCK)
        qa_s[2 * c] = _rope(res[:, :LANES], cosq_ref[...], sinq_ref[...], lo_half).astype(BF16)
        qa_s[2 * c + 1] = _rope(res[:, LANES:], cosq_ref[...], sinq_ref[...], lo_half).astype(BF16)

    def proj_kv():
        res = proj(OFF_KV)
        kr = _rope(res[:, :LANES], cosk_ref[...], sink_rot_ref[...], lo_half)
        kr_sw = pltpu.roll(kr, ATT_DIM, 1)
        va = res[:, LANES:]
        va_sw = pltpu.roll(va, ATT_DIM, 1)
        cur = slice(CHUNK, CHUNK + T)
        zero = jnp.zeros((T, LANES), F32)
        kpad_s[0, cur, :] = jnp.where(lo_head, kr, zero).astype(BF16)
        kpad_s[1, cur, :] = jnp.where(lo_head, zero, kr_sw).astype(BF16)
        kpad_s[2, cur, :] = jnp.where(lo_head, kr_sw, zero).astype(BF16)
        kpad_s[3, cur, :] = jnp.where(lo_head, zero, kr).astype(BF16)
        vpad_s[0, cur, :] = jnp.where(lo_head, va, zero).astype(BF16)
        vpad_s[1, cur, :] = jnp.where(lo_head, zero, va_sw).astype(BF16)
        vpad_s[2, cur, :] = jnp.where(lo_head, va_sw, zero).astype(BF16)
        vpad_s[3, cur, :] = jnp.where(lo_head, zero, va).astype(BF16)

    def proj_za(c):
        res = proj(OFF_ZA + c * COL_BLOCK)
        za_s[2 * c] = res[:, :LANES]
        za_s[2 * c + 1] = res[:, LANES:]

    def cumsum_pair(c):
        for ch in range(T // CHUNK):
            rows = slice(ch * CHUNK, (ch + 1) * CHUNK)
            lf = jnp.concatenate([lf_s[2 * c, rows, :], lf_s[2 * c + 1, rows, :]], axis=1)
            hi = lf.astype(BF16)
            lo = (lf - hi.astype(F32)).astype(BF16)
            b = _dot(tri_ref[...], hi) + _dot(tri_ref[...], lo)
            b_s[2 * c, rows, :] = b[:, :LANES]
            b_s[2 * c + 1, rows, :] = b[:, LANES:]
            for q0 in range(0, CHUNK, QUARTER):
                first = max(q0 - 1, 0)
                worst_spans.append(b[first:first + 1, :] - b[q0 + QUARTER - 1:q0 + QUARTER, :])

    def hgrn_load(h, ch):
        rows = pl.ds(ch * CHUNK, CHUNK)
        return rows, qh_s[h, rows, :], kh_s[h, rows, :], vh_s[h, rows, :], b_s[h, rows, :], state_s[h]

    def hgrn_finish(h, rows, q, k, v, b, st, scores, diag):
        b_last = b[CHUNK - 1:CHUNK, :]
        q_in = (q * jnp.exp2(b)).astype(BF16)
        o = _dot(jnp.concatenate([scores.astype(BF16), q_in], axis=1),
                 jnp.concatenate([v, st.T.astype(BF16)], axis=0))
        if diag is not None:
            o = o + diag * v.astype(F32)
        k_out = (k * jnp.exp2(b_last - b)).astype(BF16)
        state_s[h] = st * jnp.exp2(b_last) + _dot_tn(v, k_out)
        ms = jnp.mean(o * o, axis=-1, keepdims=True)
        on = o * lax.rsqrt(ms + NORM_EPS) * ghead
        yin_s[h, rows, :] = (on * _silu(zh_s[h, rows, :])).astype(BF16)

    def hgrn_head_robust(h, ch):
        rows, q, k, v, b, st = hgrn_load(h, ch)
        lf = lf_s[h, rows, :]
        scores = jnp.zeros((CHUNK, CHUNK), F32)
        for level in range(VREG_LEVEL):
            xk = _level_operands(q, k, b, lf, level, row_c)[0].astype(BF16)
            scores = jnp.where(lvl == level, _dot_nt(xk, xk), scores)
        prods = {}
        for level in range(VREG_LEVEL, LEVELS):
            xk, xr = _level_operands(q, k, b, lf, level, row_c)
            prods[level] = _dot_nt(xr.astype(BF16), xk.astype(BF16))
        groups = []
        for g in range(CHUNK // SUBLANES):
            rs = slice(g * SUBLANES, (g + 1) * SUBLANES)
            acc = scores[rs]
            for level in range(VREG_LEVEL, LEVELS):
                if (g * SUBLANES) & (1 << level):
                    ri = _right_group_index(g, level)
                    acc = jnp.where(lvl[rs] == level, prods[level][ri * SUBLANES:(ri + 1) * SUBLANES], acc)
            groups.append(acc)
        scores = jnp.concatenate(groups, axis=0)
        hgrn_finish(h, rows, q, k, v, b, st, scores, jnp.sum(q * k, axis=-1, keepdims=True))

    def hgrn_head_fast(h, ch):
        rows, q, k, v, b, st = hgrn_load(h, ch)
        mid_lo = jnp.broadcast_to(b[QUARTER - 1:QUARTER, :], (HALF, LANES))
        mid_hi = jnp.broadcast_to(b[HALF + QUARTER - 1:HALF + QUARTER, :], (HALF, LANES))
        d = b - jnp.concatenate([mid_lo, mid_hi], axis=0)
        q_own = (q * jnp.exp2(d)).astype(BF16)
        k_own = (k * jnp.exp2(-d)).astype(BF16)
        q_cross = (q[HALF:] * jnp.exp2(b[HALF:] - mid_lo)).astype(BF16)
        res = _dot_nt(jnp.concatenate([q_own, q_cross], axis=0), k_own)
        first_half_keys = lax.broadcasted_iota(jnp.int32, (HALF, LANES), 1) < HALF
        bottom = jnp.where(first_half_keys, res[CHUNK:], res[HALF:CHUNK])
        scores = jnp.where(causal, jnp.concatenate([res[:HALF], bottom], axis=0), 0.0)
        hgrn_finish(h, rows, q, k, v, b, st, scores, None)

    def attn_scores(n, g, pos):
        qrows = slice(n * CHUNK, (n + 1) * CHUNK)
        krows = slice(n * CHUNK, (n + 2) * CHUNK)
        pairs = range(g * PAIRS_PER_KV, (g + 1) * PAIRS_PER_KV)
        q_stack = jnp.concatenate([qa_s[p, qrows, :] for p in pairs], axis=0)
        s_all = _dot_nt(q_stack, kpad_s[2 * g + pos, krows, :])
        for i, p in enumerate(pairs):
            head = 2 * p + pos
            s = s_all[i * CHUNK:(i + 1) * CHUNK]
            s_prev = jnp.where(jnp.logical_or(j > 0, n > 0), s[:, :CHUNK], NEG_INF)
            sc = jnp.where(causal, s[:, CHUNK:], s_prev)
            sc_s[n % 2 * ATT_HEADS + head] = sc
            m = jnp.maximum(jnp.max(sc, axis=-1, keepdims=True), sink_ref[head])
            m_s[n % 2 * ATT_HEADS + head] = jnp.broadcast_to(m, (CHUNK, LANES))

    def attn_out(n, g):
        qrows = slice(n * CHUNK, (n + 1) * CHUNK)
        krows = slice(n * CHUNK, (n + 2) * CHUNK)
        pairs = range(g * PAIRS_PER_KV, (g + 1) * PAIRS_PER_KV)
        res = []
        for pos in range(2):
            for i, p in enumerate(pairs):
                head = 2 * p + pos
                pr = jnp.exp(sc_s[n % 2 * ATT_HEADS + head] - m_s[n % 2 * ATT_HEADS + head])
                pcat = jnp.concatenate([jnp.where(causal, 0.0, pr), jnp.where(causal, pr, 0.0)], axis=1)
                pst_s[pos, i * CHUNK:(i + 1) * CHUNK, :] = pcat.astype(BF16)
            vp = jnp.concatenate([vpad_s[2 * g + pos, krows, :], ones_blk], axis=1)
            res.append(_dot(pst_s[pos], vp))
        for i, p in enumerate(pairs):
            out = None
            for pos in range(2):
                head = 2 * p + pos
                r = res[pos][i * CHUNK:(i + 1) * CHUNK]
                denom = r[:, LANES:] + jnp.exp(sink_ref[head] - m_s[n % 2 * ATT_HEADS + head])
                o = r[:, :LANES] / denom
                out = o if out is None else out + o
            yin_s[HG_HEADS + p, qrows, :] = (out * _silu(za_s[p, qrows, :])).astype(BF16)

    def out_proj_hgrn(nb):
        cols = slice(nb * COL_BLOCK, (nb + 1) * COL_BLOCK)
        yin = jnp.concatenate([yin_s[i] for i in range(HG_HEADS)], axis=1)
        y_s[:, cols] = _dot(yin, wout_ref[0:HG_WIDTH, cols])

    def out_proj_attn(nb):
        cols = slice(nb * COL_BLOCK, (nb + 1) * COL_BLOCK)
        yin = jnp.concatenate([yin_s[HG_HEADS + i] for i in range(ATT_PAIRS)], axis=1)
        y_s[:, cols] = y_s[:, cols] + _dot(yin, wout_ref[HG_WIDTH:MIX_WIDTH, cols])

    def post_norm():
        for rb in range(T // ROW_BLOCK):
            rows = slice(rb * ROW_BLOCK, (rb + 1) * ROW_BLOCK)
            y = y_s[rows, :]
            ms = jnp.mean(y * y, axis=-1, keepdims=True)
            o_ref[0, rows, :] = x_ref[0, rows, :] + y * lax.rsqrt(ms + NORM_EPS) * gpost_ref[...]

    n_chunks = T // CHUNK
    n_pairs = HG_HEADS // 2
    assert ATT_WIDTH // COL_BLOCK == 2 * KV_HEADS and (n_chunks * KV_HEADS) % (D_MODEL // COL_BLOCK) == 0
    pre_norm()
    stbk_s[...] = state_s[...]
    for c in range(n_pairs + 1):
        if c < n_pairs:
            proj_steps = [lambda c=c: proj_q(c), lambda c=c: proj_f(c), lambda c=c: proj_i(c), lambda c=c: proj_z(c)]
        else:
            proj_steps = [proj_kv] + [lambda a=a: proj_qa(a) for a in range(ATT_WIDTH // COL_BLOCK)]
        mixer_steps = []
        if c > 0:
            mixer_steps = [lambda h=h, ch=ch: hgrn_head_fast(h, ch)
                           for ch in range(n_chunks) for h in (2 * c - 2, 2 * c - 1)]
        for i in range(max(len(proj_steps), len(mixer_steps))):
            if i < len(proj_steps):
                proj_steps[i]()
            if i < len(mixer_steps):
                mixer_steps[i]()
        if c < n_pairs:
            cumsum_pair(c)
    attn_units = [(g, pos) for g in range(KV_HEADS) for pos in range(2)]
    for a in range(ATT_WIDTH // COL_BLOCK):
        proj_za(a)
        attn_scores(0, *attn_units[a])
    out_every = n_chunks * KV_HEADS // (D_MODEL // COL_BLOCK)
    for n in range(n_chunks):
        for g in range(KV_HEADS):
            attn_out(n, g)
            if n + 1 < n_chunks:
                attn_scores(n + 1, g, 0)
                attn_scores(n + 1, g, 1)
            item = n * KV_HEADS + g
            if item % out_every == out_every - 1:
                out_proj_hgrn(item // out_every)

    worst = worst_spans[0]
    for span in worst_spans[1:]:
        worst = jnp.maximum(worst, span)
    fast_ok = jnp.max(worst) <= FAST_SPAN_LIMIT

    @pl.when(jnp.logical_not(fast_ok))
    def _():
        state_s[...] = stbk_s[...]
        for ch in range(n_chunks):
            def head_body(h, carry, ch=ch):
                hgrn_head_robust(h, ch)
                return carry
            lax.fori_loop(0, HG_HEADS, head_body, 0, unroll=ROBUST_HEAD_UNROLL)
        for nb in range(D_MODEL // COL_BLOCK):
            out_proj_hgrn(nb)

    kpad_s[:, 0:CHUNK, :] = kpad_s[:, T:T + CHUNK, :]
    vpad_s[:, 0:CHUNK, :] = vpad_s[:, T:T + CHUNK, :]

    for nb in range(D_MODEL // COL_BLOCK):
        out_proj_attn(nb)
    post_norm()


def _level_matrix():
    t = np.arange(CHUNK)[:, None]
    s = np.arange(CHUNK)[None, :]
    x = np.bitwise_xor(t, s)
    msb = np.where(x > 0, np.floor(np.log2(np.maximum(x, 1))), -1).astype(np.int32)
    return np.where(s < t, msb, -1).astype(np.int32)


def _rope_tables(seq):
    half = ATT_DIM // 2
    inv_freq = ROPE_THETA ** (-jnp.arange(half, dtype=F32) / half)
    ang = jnp.arange(seq, dtype=F32)[:, None] * inv_freq[None, :]
    cos = jnp.tile(jnp.cos(ang), (1, LANES // half))
    sin = jnp.sin(ang)
    sin = jnp.tile(jnp.concatenate([-sin, sin], axis=1), (1, LANES // ATT_DIM))
    return cos, sin


def _layer(x, w_in, w_out, g_pre, g_post, lb, g_head, sinks, tables):
    B, S, _ = x.shape
    T = SEQ_TILE
    cosq, sinq, cosk, sink_rot, lvl, tri = tables
    lbf = jnp.maximum(lb, LB_FLOOR)

    def const(shape):
        return pl.BlockSpec(shape, lambda b, j: (0,) * len(shape), pipeline_mode=pl.Buffered(1))

    def seq_table():
        return pl.BlockSpec((T, LANES), lambda b, j: (j, 0))

    in_specs = [
        pl.BlockSpec((1, T, D_MODEL), lambda b, j: (b, j, 0)),
        const((D_MODEL, IN_WIDTH)),
        const((MIX_WIDTH, D_MODEL)),
        const((1, D_MODEL)),
        const((1, D_MODEL)),
        const((1, HG_WIDTH)),
        const((1, HG_WIDTH)),
        const((1, HG_WIDTH)),
        const((1, HG_DIM)),
        pl.BlockSpec(memory_space=pltpu.SMEM),
        seq_table(), seq_table(), seq_table(), seq_table(),
        const((CHUNK, CHUNK)),
        const((CHUNK, CHUNK)),
    ]
    scratch = [
        pltpu.VMEM((T, D_MODEL), BF16),
        pltpu.VMEM((HG_HEADS, T, LANES), F32),
        pltpu.VMEM((HG_HEADS, T, LANES), F32),
        pltpu.VMEM((HG_HEADS, T, LANES), BF16),
        pltpu.VMEM((HG_HEADS, T, LANES), F32),
        pltpu.VMEM((HG_HEADS, T, LANES), F32),
        pltpu.VMEM((HG_HEADS, T, LANES), F32),
        pltpu.VMEM((ATT_PAIRS, T, LANES), BF16),
        pltpu.VMEM((2 * KV_HEADS, CHUNK + T, LANES), BF16),
        pltpu.VMEM((2 * KV_HEADS, CHUNK + T, LANES), BF16),
        pltpu.VMEM((ATT_PAIRS, T, LANES), F32),
        pltpu.VMEM((MIX_WIDTH // LANES, T, LANES), BF16),
        pltpu.VMEM((HG_HEADS, HG_DIM, HG_DIM), F32),
        pltpu.VMEM((2 * ATT_HEADS, CHUNK, LANES), F32),
        pltpu.VMEM((2 * ATT_HEADS, CHUNK, LANES), F32),
        pltpu.VMEM((HG_HEADS, HG_DIM, HG_DIM), F32),
        pltpu.VMEM((T, D_MODEL), F32),
        pltpu.VMEM((2, PAIRS_PER_KV * CHUNK, 2 * CHUNK), BF16),
    ]
    return pl.pallas_call(
        _layer_kernel,
        out_shape=jax.ShapeDtypeStruct(x.shape, x.dtype),
        grid=(B, S // T),
        in_specs=in_specs,
        out_specs=pl.BlockSpec((1, T, D_MODEL), lambda b, j: (b, j, 0)),
        scratch_shapes=scratch,
        compiler_params=pltpu.CompilerParams(
            dimension_semantics=("arbitrary", "arbitrary"),
            vmem_limit_bytes=VMEM_LIMIT_BYTES),
        name="hybrid_layer",
    )(x, w_in.astype(BF16), w_out.astype(BF16), g_pre[None, :], g_post[None, :],
      lbf[None, :], (1.0 - lb)[None, :], (lbf - lb)[None, :], g_head[None, :], sinks,
      cosq, sinq, cosk, sink_rot, lvl, tri)


def kernel(x, w_in, w_out, g_pre, g_post, lb_param, g_head, sinks):
    depth = w_in.shape[0]
    seq = x.shape[1]
    p = jax.nn.softmax(lb_param.astype(F32), axis=0)
    lower_bounds = jnp.cumsum(p, axis=0) - p[0:1]
    cos, sin = _rope_tables(seq)
    tri = jnp.asarray(np.tril(np.ones((CHUNK, CHUNK), np.float32)), BF16)
    tables = (cos * ATT_SCALE, sin * ATT_SCALE, cos, sin, jnp.asarray(_level_matrix()), tri)
    for l in range(depth):
        x = _layer(x, w_in[l], w_out[l], g_pre[l], g_post[l], lower_bounds[l], g_head[l], sinks[l], tables)
    return x
```

```python
import math

import numpy as np
import jax
import jax.numpy as jnp
from jax import lax
from jax.experimental import pallas as pl
from jax.experimental.pallas import tpu as pltpu

D_MODEL = 1024
HG_WIDTH = 1024
HG_HEADS = 8
HG_DIM = 128
ATT_WIDTH = 1024
ATT_HEADS = 16
ATT_DIM = 64
ATT_PAIRS = ATT_HEADS // 2
KV_HEADS = 2
KV_WIDTH = 128
PAIRS_PER_KV = ATT_PAIRS // KV_HEADS
WINDOW = 128
IN_WIDTH = 6400
MIX_WIDTH = 2048
ATT_SCALE = 1.0 / math.sqrt(ATT_DIM)
ROPE_THETA = 10000.0
NORM_EPS = 1e-6
NEG_INF = -1e30
LB_FLOOR = 1e-20

LANES = 128
SUBLANES = 8
CHUNK = 128
HALF = CHUNK // 2
QUARTER = CHUNK // 4
FAST_SPAN_LIMIT = 64.0
LEVELS = 7
VREG_LEVEL = 3
SEQ_TILE = 256
COL_BLOCK = 256
ROW_BLOCK = 64
ROBUST_HEAD_UNROLL = 4
HEAD_GROUP = 4
MIX_SPACING = 1
VMEM_LIMIT_BYTES = 56 * 1024 * 1024

OFF_QH, OFF_F, OFF_I, OFF_ZH, OFF_QA, OFF_KV, OFF_ZA = 0, 1024, 2048, 3072, 4096, 5120, 5376

F32 = jnp.float32
BF16 = jnp.bfloat16


def _dot(a, b):
    return jnp.dot(a, b, preferred_element_type=F32)


def _dot_nt(a, b):
    return lax.dot_general(a, b, (((1,), (1,)), ((), ())), preferred_element_type=F32)


def _dot_tn(a, b):
    return lax.dot_general(a, b, (((0,), (0,)), ((), ())), preferred_element_type=F32)


def _silu(x):
    return x * jax.nn.sigmoid(x)


def _rope(x, cos, sin, lo_half):
    swapped = jnp.where(lo_half, pltpu.roll(x, LANES - ATT_DIM // 2, 1), pltpu.roll(x, ATT_DIM // 2, 1))
    return x * cos + swapped * sin


def _hold_rows(b, period, offset):
    pieces = []
    for blk in range(CHUNK // period):
        row = blk * period + offset
        pieces.append(jnp.broadcast_to(b[row:row + 1, :], (period, LANES)))
    return pieces[0] if len(pieces) == 1 else jnp.concatenate(pieces, axis=0)


def _neg_abs(x):
    return -jnp.abs(x)


def _level_operands(q, k, b, lf, level, row_idx):
    half = 1 << level
    if level >= VREG_LEVEL:
        left, right = [], []
        for blk in range(CHUNK // (2 * half)):
            lo, mid, hi = blk * 2 * half, blk * 2 * half + half, (blk + 1) * 2 * half
            r = jnp.broadcast_to(b[mid - 1:mid, :], (half, LANES))
            left.append(k[lo:mid] * jnp.exp2(r - b[lo:mid]))
            right.append(q[mid:hi] * jnp.exp2(b[mid:hi] - r))
        both = [piece for pair in zip(left, right) for piece in pair]
        return jnp.concatenate(both, axis=0), (right[0] if len(right) == 1 else jnp.concatenate(right, axis=0))
    if half == 4:
        arg = _neg_abs(b - _hold_rows(b, 8, 3))
    elif half == 2:
        arg = _neg_abs(b - jnp.where((row_idx & 7) < 4, _hold_rows(b, 8, 1), _hold_rows(b, 8, 5)))
    else:
        arg = jnp.where((row_idx & 1) == 1, lf, 0.0)
    is_right = (row_idx & (2 * half - 1)) >= half
    return jnp.where(is_right, q, k) * jnp.exp2(arg), None


def _right_group_index(group, level):
    half_groups = (1 << level) // SUBLANES
    return (group // (2 * half_groups)) * half_groups + group % half_groups


def _layer_kernel(x_ref, win_ref, wout_ref, gpre_ref, gpost_ref, lbf_ref, oml_ref, lbd_ref, ghead_ref,
                  sink_ref, cosq_ref, sinq_ref, cosk_ref, sink_rot_ref, lvl_ref, tri_ref,
                  o_ref,
                  hn_s, qh_s, kh_s, vh_s, lf_s, b_s, zh_s, qa_s, kpad_s, vpad_s, za_s, yin_s, state_s,
                  sc_s, m_s, stbk_s, y_s, pst_s):
    T = SEQ_TILE
    j = pl.program_id(1)

    @pl.when(j == 0)
    def _():
        state_s[...] = jnp.zeros_like(state_s)
        kpad_s[:, 0:CHUNK, :] = jnp.zeros((2 * KV_HEADS, CHUNK, LANES), BF16)
        vpad_s[:, 0:CHUNK, :] = jnp.zeros((2 * KV_HEADS, CHUNK, LANES), BF16)

    lane = lax.broadcasted_iota(jnp.int32, (T, LANES), 1)
    lo_half = (lane & (ATT_DIM - 1)) < ATT_DIM // 2
    lo_head = lane < ATT_DIM
    row_c = lax.broadcasted_iota(jnp.int32, (CHUNK, LANES), 0)
    col_c = lax.broadcasted_iota(jnp.int32, (CHUNK, LANES), 1)
    causal = row_c >= col_c
    lvl = lvl_ref[...]
    ghead = ghead_ref[...]
    ones_blk = jnp.ones((2 * CHUNK, LANES), BF16)
    worst_spans = []

    def pre_norm():
        for rb in range(T // ROW_BLOCK):
            rows = slice(rb * ROW_BLOCK, (rb + 1) * ROW_BLOCK)
            x = x_ref[0, rows, :]
            ms = jnp.mean(x * x, axis=-1, keepdims=True)
            hn_s[rows, :] = (x * lax.rsqrt(ms + NORM_EPS) * gpre_ref[...]).astype(BF16)

    def proj(col):
        return _dot(hn_s[...], win_ref[:, col:col + COL_BLOCK])

    def proj_q(c):
        qf = _silu(proj(OFF_QH + c * COL_BLOCK))
        qh_s[2 * c] = qf[:, :LANES]
        qh_s[2 * c + 1] = qf[:, LANES:]

    def proj_f(c):
        cols = slice(c * COL_BLOCK, (c + 1) * COL_BLOCK)
        sig = jax.nn.sigmoid(proj(OFF_F + c * COL_BLOCK))
        oml = oml_ref[:, cols]
        lf = jnp.log2(lbf_ref[:, cols] + oml * sig)
        k = oml * (1.0 - sig) - lbd_ref[:, cols]
        lf_s[2 * c] = lf[:, :LANES]
        lf_s[2 * c + 1] = lf[:, LANES:]
        kh_s[2 * c] = k[:, :LANES]
        kh_s[2 * c + 1] = k[:, LANES:]

    def proj_i(c):
        res = proj(OFF_I + c * COL_BLOCK).astype(BF16)
        vh_s[2 * c] = res[:, :LANES]
        vh_s[2 * c + 1] = res[:, LANES:]

    def proj_z(c):
        res = proj(OFF_ZH + c * COL_BLOCK)
        zh_s[2 * c] = res[:, :LANES]
        zh_s[2 * c + 1] = res[:, LANES:]

    def proj_qa(c):
        res = proj(OFF_QA + c * COL_BLOCK)
        qa_s[2 * c] = _rope(res[:, :LANES], cosq_ref[...], sinq_ref[...], lo_half).astype(BF16)
        qa_s[2 * c + 1] = _rope(res[:, LANES:], cosq_ref[...], sinq_ref[...], lo_half).astype(BF16)

    def proj_kv():
        res = proj(OFF_KV)
        kr = _rope(res[:, :LANES], cosk_ref[...], sink_rot_ref[...], lo_half)
        kr_sw = pltpu.roll(kr, ATT_DIM, 1)
        va = res[:, LANES:]
        va_sw = pltpu.roll(va, ATT_DIM, 1)
        cur = slice(CHUNK, CHUNK + T)
        zero = jnp.zeros((T, LANES), F32)
        kpad_s[0, cur, :] = jnp.where(lo_head, kr, zero).astype(BF16)
        kpad_s[1, cur, :] = jnp.where(lo_head, zero, kr_sw).astype(BF16)
        kpad_s[2, cur, :] = jnp.where(lo_head, kr_sw, zero).astype(BF16)
        kpad_s[3, cur, :] = jnp.where(lo_head, zero, kr).astype(BF16)
        vpad_s[0, cur, :] = jnp.where(lo_head, va, zero).astype(BF16)
        vpad_s[1, cur, :] = jnp.where(lo_head, zero, va_sw).astype(BF16)
        vpad_s[2, cur, :] = jnp.where(lo_head, va_sw, zero).astype(BF16)
        vpad_s[3, cur, :] = jnp.where(lo_head, zero, va).astype(BF16)

    def proj_za(c):
        res = proj(OFF_ZA + c * COL_BLOCK)
        za_s[2 * c] = res[:, :LANES]
        za_s[2 * c + 1] = res[:, LANES:]

    def cumsum_pair(c):
        for ch in range(T // CHUNK):
            rows = slice(ch * CHUNK, (ch + 1) * CHUNK)
            lf = jnp.concatenate([lf_s[2 * c, rows, :], lf_s[2 * c + 1, rows, :]], axis=1)
            hi = lf.astype(BF16)
            lo = (lf - hi.astype(F32)).astype(BF16)
            b = _dot(tri_ref[...], hi) + _dot(tri_ref[...], lo)
            b_s[2 * c, rows, :] = b[:, :LANES]
            b_s[2 * c + 1, rows, :] = b[:, LANES:]
            for q0 in range(0, CHUNK, QUARTER):
                first = max(q0 - 1, 0)
                worst_spans.append(b[first:first + 1, :] - b[q0 + QUARTER - 1:q0 + QUARTER, :])

    def hgrn_load(h, ch):
        rows = pl.ds(ch * CHUNK, CHUNK)
        return rows, qh_s[h, rows, :], kh_s[h, rows, :], vh_s[h, rows, :], b_s[h, rows, :], state_s[h]

    def hgrn_finish(h, rows, q, k, v, b, st, scores, diag):
        b_last = b[CHUNK - 1:CHUNK, :]
        q_in = (q * jnp.exp2(b)).astype(BF16)
        o = _dot(scores.astype(BF16), v) + _dot_nt(q_in, st.astype(BF16))
        if diag is not None:
            o = o + diag * v.astype(F32)
        k_out = (k * jnp.exp2(b_last - b)).astype(BF16)
        state_s[h] = st * jnp.exp2(b_last) + _dot_tn(v, k_out)
        ms = jnp.mean(o * o, axis=-1, keepdims=True)
        on = o * lax.rsqrt(ms + NORM_EPS) * ghead
        yin_s[h, rows, :] = (on * _silu(zh_s[h, rows, :])).astype(BF16)

    def hgrn_head_robust(h, ch):
        rows, q, k, v, b, st = hgrn_load(h, ch)
        lf = lf_s[h, rows, :]
        scores = jnp.zeros((CHUNK, CHUNK), F32)
        for level in range(VREG_LEVEL):
            xk = _level_operands(q, k, b, lf, level, row_c)[0].astype(BF16)
            scores = jnp.where(lvl == level, _dot_nt(xk, xk), scores)
        prods = {}
        for level in range(VREG_LEVEL, LEVELS):
            xk, xr = _level_operands(q, k, b, lf, level, row_c)
            prods[level] = _dot_nt(xr.astype(BF16), xk.astype(BF16))
        groups = []
        for g in range(CHUNK // SUBLANES):
            rs = slice(g * SUBLANES, (g + 1) * SUBLANES)
            acc = scores[rs]
            for level in range(VREG_LEVEL, LEVELS):
                if (g * SUBLANES) & (1 << level):
                    ri = _right_group_index(g, level)
                    acc = jnp.where(lvl[rs] == level, prods[level][ri * SUBLANES:(ri + 1) * SUBLANES], acc)
            groups.append(acc)
        scores = jnp.concatenate(groups, axis=0)
        hgrn_finish(h, rows, q, k, v, b, st, scores, jnp.sum(q * k, axis=-1, keepdims=True))

    def hgrn_fast_scores(h, ch):
        rows, q, k, v, b, st = hgrn_load(h, ch)
        mid_lo = jnp.broadcast_to(b[QUARTER - 1:QUARTER, :], (HALF, LANES))
        mid_hi = jnp.broadcast_to(b[HALF + QUARTER - 1:HALF + QUARTER, :], (HALF, LANES))
        d = b - jnp.concatenate([mid_lo, mid_hi], axis=0)
        q_own = (q * jnp.exp2(d)).astype(BF16)
        k_own = (k * jnp.exp2(-d)).astype(BF16)
        q_cross = (q[HALF:] * jnp.exp2(b[HALF:] - mid_lo)).astype(BF16)
        res = _dot_nt(jnp.concatenate([q_own, q_cross], axis=0), k_own)
        b_last = b[CHUNK - 1:CHUNK, :]
        return dict(rows=rows, res=res, v=v, st=st, q_in=(q * jnp.exp2(b)).astype(BF16),
                    k_out=(k * jnp.exp2(b_last - b)).astype(BF16), keep=jnp.exp2(b_last))

    def hgrn_fast_mix(h, t):
        res, v, st = t["res"], t["v"], t["st"]
        first_half_keys = lax.broadcasted_iota(jnp.int32, (HALF, LANES), 1) < HALF
        bottom = jnp.where(first_half_keys, res[CHUNK:], res[HALF:CHUNK])
        scores = jnp.where(causal, jnp.concatenate([res[:HALF], bottom], axis=0), 0.0)
        o = _dot(scores.astype(BF16), v) + _dot_nt(t["q_in"], st.astype(BF16))
        state_s[h] = st * t["keep"] + _dot_tn(v, t["k_out"])
        return o

    def hgrn_fast_gate(h, t, o):
        ms = jnp.mean(o * o, axis=-1, keepdims=True)
        on = o * lax.rsqrt(ms + NORM_EPS) * ghead
        yin_s[h, t["rows"], :] = (on * _silu(zh_s[h, t["rows"], :])).astype(BF16)

    def hgrn_heads_fast(heads, ch):
        parts = [hgrn_fast_scores(h, ch) for h in heads]
        outs = [hgrn_fast_mix(h, t) for h, t in zip(heads, parts)]
        for h, t, o in zip(heads, parts, outs):
            hgrn_fast_gate(h, t, o)

    def attn_scores(n, g, pos):
        qrows = slice(n * CHUNK, (n + 1) * CHUNK)
        krows = slice(n * CHUNK, (n + 2) * CHUNK)
        pairs = range(g * PAIRS_PER_KV, (g + 1) * PAIRS_PER_KV)
        q_stack = jnp.concatenate([qa_s[p, qrows, :] for p in pairs], axis=0)
        s_all = _dot_nt(q_stack, kpad_s[2 * g + pos, krows, :])
        for i, p in enumerate(pairs):
            head = 2 * p + pos
            s = s_all[i * CHUNK:(i + 1) * CHUNK]
            s_prev = jnp.where(jnp.logical_or(j > 0, n > 0), s[:, :CHUNK], NEG_INF)
            sc = jnp.where(causal, s[:, CHUNK:], s_prev)
            sc_s[n % 2 * ATT_HEADS + head] = sc
            m = jnp.maximum(jnp.max(sc, axis=-1, keepdims=True), sink_ref[head])
            m_s[n % 2 * ATT_HEADS + head] = jnp.broadcast_to(m, (CHUNK, LANES))

    def attn_out(n, g):
        qrows = slice(n * CHUNK, (n + 1) * CHUNK)
        krows = slice(n * CHUNK, (n + 2) * CHUNK)
        pairs = range(g * PAIRS_PER_KV, (g + 1) * PAIRS_PER_KV)
        res = []
        for pos in range(2):
            for i, p in enumerate(pairs):
                head = 2 * p + pos
                pr = jnp.exp(sc_s[n % 2 * ATT_HEADS + head] - m_s[n % 2 * ATT_HEADS + head])
                pcat = jnp.concatenate([jnp.where(causal, 0.0, pr), jnp.where(causal, pr, 0.0)], axis=1)
                pst_s[pos, i * CHUNK:(i + 1) * CHUNK, :] = pcat.astype(BF16)
            vp = jnp.concatenate([vpad_s[2 * g + pos, krows, :], ones_blk], axis=1)
            res.append(_dot(pst_s[pos], vp))
        for i, p in enumerate(pairs):
            out = None
            for pos in range(2):
                head = 2 * p + pos
                r = res[pos][i * CHUNK:(i + 1) * CHUNK]
                denom = r[:, LANES:] + jnp.exp(sink_ref[head] - m_s[n % 2 * ATT_HEADS + head])
                o = r[:, :LANES] / denom
                out = o if out is None else out + o
            yin_s[HG_HEADS + p, qrows, :] = (out * _silu(za_s[p, qrows, :])).astype(BF16)

    def out_proj_hgrn(nb):
        cols = slice(nb * COL_BLOCK, (nb + 1) * COL_BLOCK)
        yin = jnp.concatenate([yin_s[i] for i in range(HG_HEADS)], axis=1)
        y_s[:, cols] = _dot(yin, wout_ref[0:HG_WIDTH, cols])

    def out_proj_attn(nb):
        cols = slice(nb * COL_BLOCK, (nb + 1) * COL_BLOCK)
        yin = jnp.concatenate([yin_s[HG_HEADS + i] for i in range(ATT_PAIRS)], axis=1)
        y_s[:, cols] = y_s[:, cols] + _dot(yin, wout_ref[HG_WIDTH:MIX_WIDTH, cols])

    def post_norm():
        for rb in range(T // ROW_BLOCK):
            rows = slice(rb * ROW_BLOCK, (rb + 1) * ROW_BLOCK)
            y = y_s[rows, :]
            ms = jnp.mean(y * y, axis=-1, keepdims=True)
            o_ref[0, rows, :] = x_ref[0, rows, :] + y * lax.rsqrt(ms + NORM_EPS) * gpost_ref[...]

    n_chunks = T // CHUNK
    n_pairs = HG_HEADS // 2
    assert ATT_WIDTH // COL_BLOCK == 2 * KV_HEADS and (n_chunks * KV_HEADS) % (D_MODEL // COL_BLOCK) == 0
    pre_norm()
    stbk_s[...] = state_s[...]
    proj_items = []
    for c in range(n_pairs):
        proj_items += [lambda c=c: proj_q(c), lambda c=c: proj_f(c), lambda c=c: proj_i(c),
                       lambda c=c: (proj_z(c), cumsum_pair(c))]
    proj_items += [proj_kv] + [lambda a=a: proj_qa(a) for a in range(ATT_WIDTH // COL_BLOCK)]
    mixer_at = {}
    for grp in range(HG_HEADS // HEAD_GROUP):
        heads = tuple(range(grp * HEAD_GROUP, (grp + 1) * HEAD_GROUP))
        ready = 4 * (HEAD_GROUP // 2) * (grp + 1) - 1
        for ch in range(n_chunks):
            mixer_at[ready + MIX_SPACING * (ch + 1)] = (heads, ch)
    assert max(mixer_at) < len(proj_items)
    for i, item in enumerate(proj_items):
        item()
        if i in mixer_at:
            hgrn_heads_fast(*mixer_at[i])
    attn_units = [(g, pos) for g in range(KV_HEADS) for pos in range(2)]
    for a in range(ATT_WIDTH // COL_BLOCK):
        proj_za(a)
        attn_scores(0, *attn_units[a])
    out_every = n_chunks * KV_HEADS // (D_MODEL // COL_BLOCK)
    for n in range(n_chunks):
        for g in range(KV_HEADS):
            attn_out(n, g)
            if n + 1 < n_chunks:
                attn_scores(n + 1, g, 0)
                attn_scores(n + 1, g, 1)
            item = n * KV_HEADS + g
            if item % out_every == out_every - 1:
                out_proj_hgrn(item // out_every)

    worst = worst_spans[0]
    for span in worst_spans[1:]:
        worst = jnp.maximum(worst, span)
    fast_ok = jnp.max(worst) <= FAST_SPAN_LIMIT

    @pl.when(jnp.logical_not(fast_ok))
    def _():
        state_s[...] = stbk_s[...]
        for ch in range(n_chunks):
            def head_body(h, carry, ch=ch):
                hgrn_head_robust(h, ch)
                return carry
            lax.fori_loop(0, HG_HEADS, head_body, 0, unroll=ROBUST_HEAD_UNROLL)
        for nb in range(D_MODEL // COL_BLOCK):
            out_proj_hgrn(nb)

    kpad_s[:, 0:CHUNK, :] = kpad_s[:, T:T + CHUNK, :]
    vpad_s[:, 0:CHUNK, :] = vpad_s[:, T:T + CHUNK, :]

    for nb in range(D_MODEL // COL_BLOCK):
        out_proj_attn(nb)
    post_norm()


def _level_matrix():
    t = np.arange(CHUNK)[:, None]
    s = np.arange(CHUNK)[None, :]
    x = np.bitwise_xor(t, s)
    msb = np.where(x > 0, np.floor(np.log2(np.maximum(x, 1))), -1).astype(np.int32)
    return np.where(s < t, msb, -1).astype(np.int32)


def _rope_tables(seq):
    half = ATT_DIM // 2
    inv_freq = ROPE_THETA ** (-jnp.arange(half, dtype=F32) / half)
    ang = jnp.arange(seq, dtype=F32)[:, None] * inv_freq[None, :]
    cos = jnp.tile(jnp.cos(ang), (1, LANES // half))
    sin = jnp.sin(ang)
    sin = jnp.tile(jnp.concatenate([-sin, sin], axis=1), (1, LANES // ATT_DIM))
    return cos, sin


def _layer(x, layer, w_in, w_out, g_pre, g_post, lb, g_head, sinks, tables):
    B, S, _ = x.shape
    T = SEQ_TILE
    cosq, sinq, cosk, sink_rot, lvl, tri = tables
    lbf = jnp.maximum(lb, LB_FLOOR)

    def const(shape):
        return pl.BlockSpec(shape, lambda b, j: (0,) * len(shape), pipeline_mode=pl.Buffered(1))

    def layer_slab(shape):
        return pl.BlockSpec((None,) + shape, lambda b, j: (layer,) + (0,) * len(shape), pipeline_mode=pl.Buffered(1))

    def seq_table():
        return pl.BlockSpec((T, LANES), lambda b, j: (j, 0))

    in_specs = [
        pl.BlockSpec((1, T, D_MODEL), lambda b, j: (b, j, 0)),
        layer_slab((D_MODEL, IN_WIDTH)),
        layer_slab((MIX_WIDTH, D_MODEL)),
        const((1, D_MODEL)),
        const((1, D_MODEL)),
        const((1, HG_WIDTH)),
        const((1, HG_WIDTH)),
        const((1, HG_WIDTH)),
        const((1, HG_DIM)),
        pl.BlockSpec(memory_space=pltpu.SMEM),
        seq_table(), seq_table(), seq_table(), seq_table(),
        const((CHUNK, CHUNK)),
        const((CHUNK, CHUNK)),
    ]
    scratch = [
        pltpu.VMEM((T, D_MODEL), BF16),
        pltpu.VMEM((HG_HEADS, T, LANES), F32),
        pltpu.VMEM((HG_HEADS, T, LANES), F32),
        pltpu.VMEM((HG_HEADS, T, LANES), BF16),
        pltpu.VMEM((HG_HEADS, T, LANES), F32),
        pltpu.VMEM((HG_HEADS, T, LANES), F32),
        pltpu.VMEM((HG_HEADS, T, LANES), F32),
        pltpu.VMEM((ATT_PAIRS, T, LANES), BF16),
        pltpu.VMEM((2 * KV_HEADS, CHUNK + T, LANES), BF16),
        pltpu.VMEM((2 * KV_HEADS, CHUNK + T, LANES), BF16),
        pltpu.VMEM((ATT_PAIRS, T, LANES), F32),
        pltpu.VMEM((MIX_WIDTH // LANES, T, LANES), BF16),
        pltpu.VMEM((HG_HEADS, HG_DIM, HG_DIM), F32),
        pltpu.VMEM((2 * ATT_HEADS, CHUNK, LANES), F32),
        pltpu.VMEM((2 * ATT_HEADS, CHUNK, LANES), F32),
        pltpu.VMEM((HG_HEADS, HG_DIM, HG_DIM), F32),
        pltpu.VMEM((T, D_MODEL), F32),
        pltpu.VMEM((2, PAIRS_PER_KV * CHUNK, 2 * CHUNK), BF16),
    ]
    return pl.pallas_call(
        _layer_kernel,
        out_shape=jax.ShapeDtypeStruct(x.shape, x.dtype),
        grid=(B, S // T),
        in_specs=in_specs,
        out_specs=pl.BlockSpec((1, T, D_MODEL), lambda b, j: (b, j, 0)),
        scratch_shapes=scratch,
        compiler_params=pltpu.CompilerParams(
            dimension_semantics=("arbitrary", "arbitrary"),
            vmem_limit_bytes=VMEM_LIMIT_BYTES),
        name="hybrid_layer",
    )(x, w_in, w_out, g_pre[None, :], g_post[None, :],
      lbf[None, :], (1.0 - lb)[None, :], (lbf - lb)[None, :], g_head[None, :], sinks,
      cosq, sinq, cosk, sink_rot, lvl, tri)


def kernel(x, w_in, w_out, g_pre, g_post, lb_param, g_head, sinks):
    depth = w_in.shape[0]
    seq = x.shape[1]
    p = jax.nn.softmax(lb_param.astype(F32), axis=0)
    lower_bounds = jnp.cumsum(p, axis=0) - p[0:1]
    cos, sin = _rope_tables(seq)
    tri = jnp.asarray(np.tril(np.ones((CHUNK, CHUNK), np.float32)), BF16)
    tables = (cos * ATT_SCALE, sin * ATT_SCALE, cos, sin, jnp.asarray(_level_matrix()), tri)
    w_in, w_out = w_in.astype(BF16), w_out.astype(BF16)
    for l in range(depth):
        x = _layer(x, l, w_in, w_out, g_pre[l], g_post[l], lower_bounds[l], g_head[l], sinks[l], tables)
    return x
```

```python
import math

import numpy as np
import jax
import jax.numpy as jnp
from jax import lax
from jax.experimental import pallas as pl
from jax.experimental.pallas import tpu as pltpu

D_MODEL = 1024
HG_WIDTH = 1024
HG_HEADS = 8
HG_DIM = 128
ATT_WIDTH = 1024
ATT_HEADS = 16
ATT_DIM = 64
ATT_PAIRS = ATT_HEADS // 2
KV_HEADS = 2
KV_WIDTH = 128
PAIRS_PER_KV = ATT_PAIRS // KV_HEADS
WINDOW = 128
IN_WIDTH = 6400
MIX_WIDTH = 2048
ATT_SCALE = 1.0 / math.sqrt(ATT_DIM)
ROPE_THETA = 10000.0
NORM_EPS = 1e-6
NEG_INF = -1e30
LB_FLOOR = 1e-20

LANES = 128
SUBLANES = 8
CHUNK = 128
HALF = CHUNK // 2
QUARTER = CHUNK // 4
FAST_SPAN_LIMIT = 64.0
LEVELS = 7
VREG_LEVEL = 3
SEQ_TILE = 256
COL_BLOCK = 256
ROW_BLOCK = 64
ROBUST_HEAD_UNROLL = 4
HEAD_GROUP = 4
MIX_SPACING = 1
VMEM_LIMIT_BYTES = 56 * 1024 * 1024

OFF_QH, OFF_F, OFF_I, OFF_ZH, OFF_QA, OFF_KV, OFF_ZA = 0, 1024, 2048, 3072, 4096, 5120, 5376

F32 = jnp.float32
BF16 = jnp.bfloat16


def _dot(a, b):
    return jnp.dot(a, b, preferred_element_type=F32)


def _dot_nt(a, b):
    return lax.dot_general(a, b, (((1,), (1,)), ((), ())), preferred_element_type=F32)


def _dot_tn(a, b):
    return lax.dot_general(a, b, (((0,), (0,)), ((), ())), preferred_element_type=F32)


def _silu(x):
    return x * jax.nn.sigmoid(x)


def _rope(x, cos, sin, lo_half):
    swapped = jnp.where(lo_half, pltpu.roll(x, LANES - ATT_DIM // 2, 1), pltpu.roll(x, ATT_DIM // 2, 1))
    return x * cos + swapped * sin


def _hold_rows(b, period, offset):
    pieces = []
    for blk in range(CHUNK // period):
        row = blk * period + offset
        pieces.append(jnp.broadcast_to(b[row:row + 1, :], (period, LANES)))
    return pieces[0] if len(pieces) == 1 else jnp.concatenate(pieces, axis=0)


def _neg_abs(x):
    return -jnp.abs(x)


def _level_operands(q, k, b, lf, level, row_idx):
    half = 1 << level
    if level >= VREG_LEVEL:
        left, right = [], []
        for blk in range(CHUNK // (2 * half)):
            lo, mid, hi = blk * 2 * half, blk * 2 * half + half, (blk + 1) * 2 * half
            r = jnp.broadcast_to(b[mid - 1:mid, :], (half, LANES))
            left.append(k[lo:mid] * jnp.exp2(r - b[lo:mid]))
            right.append(q[mid:hi] * jnp.exp2(b[mid:hi] - r))
        both = [piece for pair in zip(left, right) for piece in pair]
        return jnp.concatenate(both, axis=0), (right[0] if len(right) == 1 else jnp.concatenate(right, axis=0))
    if half == 4:
        arg = _neg_abs(b - _hold_rows(b, 8, 3))
    elif half == 2:
        arg = _neg_abs(b - jnp.where((row_idx & 7) < 4, _hold_rows(b, 8, 1), _hold_rows(b, 8, 5)))
    else:
        arg = jnp.where((row_idx & 1) == 1, lf, 0.0)
    is_right = (row_idx & (2 * half - 1)) >= half
    return jnp.where(is_right, q, k) * jnp.exp2(arg), None


def _right_group_index(group, level):
    half_groups = (1 << level) // SUBLANES
    return (group // (2 * half_groups)) * half_groups + group % half_groups


def _layer_kernel(x_ref, win_ref, wout_ref, gpre_ref, gpost_ref, lbf_ref, oml_ref, lbd_ref, ghead_ref,
                  sink_ref, cosq_ref, sinq_ref, cosk_ref, sink_rot_ref, lvl_ref, tri_ref,
                  o_ref,
                  hn_s, qh_s, kh_s, vh_s, lf_s, b_s, zh_s, qa_s, kpad_s, vpad_s, za_s, yin_s, state_s,
                  sc_s, m_s, stbk_s, y_s, pst_s):
    T = SEQ_TILE
    j = pl.program_id(1)

    @pl.when(j == 0)
    def _():
        state_s[...] = jnp.zeros_like(state_s)
        kpad_s[:, 0:CHUNK, :] = jnp.zeros((2 * KV_HEADS, CHUNK, LANES), BF16)
        vpad_s[:, 0:CHUNK, :] = jnp.zeros((2 * KV_HEADS, CHUNK, LANES), BF16)

    lane = lax.broadcasted_iota(jnp.int32, (T, LANES), 1)
    lo_half = (lane & (ATT_DIM - 1)) < ATT_DIM // 2
    lo_head = lane < ATT_DIM
    row_c = lax.broadcasted_iota(jnp.int32, (CHUNK, LANES), 0)
    col_c = lax.broadcasted_iota(jnp.int32, (CHUNK, LANES), 1)
    causal = row_c >= col_c
    lvl = lvl_ref[...]
    ghead = ghead_ref[...]
    ones_blk = jnp.ones((2 * CHUNK, LANES), BF16)
    worst_spans = []

    def pre_norm():
        for rb in range(T // ROW_BLOCK):
            rows = slice(rb * ROW_BLOCK, (rb + 1) * ROW_BLOCK)
            x = x_ref[0, rows, :]
            ms = jnp.mean(x * x, axis=-1, keepdims=True)
            hn_s[rows, :] = (x * lax.rsqrt(ms + NORM_EPS) * gpre_ref[...]).astype(BF16)

    def proj(col):
        return _dot(hn_s[...], win_ref[:, col:col + COL_BLOCK])

    def proj_q(c):
        qf = _silu(proj(OFF_QH + c * COL_BLOCK))
        qh_s[2 * c] = qf[:, :LANES]
        qh_s[2 * c + 1] = qf[:, LANES:]

    def proj_f(c):
        cols = slice(c * COL_BLOCK, (c + 1) * COL_BLOCK)
        sig = jax.nn.sigmoid(proj(OFF_F + c * COL_BLOCK))
        oml = oml_ref[:, cols]
        lf = jnp.log2(lbf_ref[:, cols] + oml * sig)
        k = oml * (1.0 - sig) - lbd_ref[:, cols]
        lf_s[2 * c] = lf[:, :LANES]
        lf_s[2 * c + 1] = lf[:, LANES:]
        kh_s[2 * c] = k[:, :LANES]
        kh_s[2 * c + 1] = k[:, LANES:]

    def proj_i(c):
        res = proj(OFF_I + c * COL_BLOCK).astype(BF16)
        vh_s[2 * c] = res[:, :LANES]
        vh_s[2 * c + 1] = res[:, LANES:]

    def proj_z(c):
        res = proj(OFF_ZH + c * COL_BLOCK)
        zh_s[2 * c] = res[:, :LANES]
        zh_s[2 * c + 1] = res[:, LANES:]

    def proj_qa(c):
        res = proj(OFF_QA + c * COL_BLOCK)
        qa_s[2 * c] = _rope(res[:, :LANES], cosq_ref[...], sinq_ref[...], lo_half).astype(BF16)
        qa_s[2 * c + 1] = _rope(res[:, LANES:], cosq_ref[...], sinq_ref[...], lo_half).astype(BF16)

    def proj_kv():
        res = proj(OFF_KV)
        kr = _rope(res[:, :LANES], cosk_ref[...], sink_rot_ref[...], lo_half)
        kr_sw = pltpu.roll(kr, ATT_DIM, 1)
        va = res[:, LANES:]
        va_sw = pltpu.roll(va, ATT_DIM, 1)
        cur = slice(CHUNK, CHUNK + T)
        zero = jnp.zeros((T, LANES), F32)
        kpad_s[0, cur, :] = jnp.where(lo_head, kr, zero).astype(BF16)
        kpad_s[1, cur, :] = jnp.where(lo_head, zero, kr_sw).astype(BF16)
        kpad_s[2, cur, :] = jnp.where(lo_head, kr_sw, zero).astype(BF16)
        kpad_s[3, cur, :] = jnp.where(lo_head, zero, kr).astype(BF16)
        vpad_s[0, cur, :] = jnp.where(lo_head, va, zero).astype(BF16)
        vpad_s[1, cur, :] = jnp.where(lo_head, zero, va_sw).astype(BF16)
        vpad_s[2, cur, :] = jnp.where(lo_head, va_sw, zero).astype(BF16)
        vpad_s[3, cur, :] = jnp.where(lo_head, zero, va).astype(BF16)

    def proj_za(c):
        res = proj(OFF_ZA + c * COL_BLOCK)
        za_s[2 * c] = res[:, :LANES]
        za_s[2 * c + 1] = res[:, LANES:]

    def cumsum_pair(c):
        for ch in range(T // CHUNK):
            rows = slice(ch * CHUNK, (ch + 1) * CHUNK)
            lf = jnp.concatenate([lf_s[2 * c, rows, :], lf_s[2 * c + 1, rows, :]], axis=1)
            hi = lf.astype(BF16)
            lo = (lf - hi.astype(F32)).astype(BF16)
            b = _dot(tri_ref[...], jnp.concatenate([hi, lo], axis=0))
            b_s[2 * c, rows, :] = b[:, :LANES]
            b_s[2 * c + 1, rows, :] = b[:, LANES:]
            for q0 in range(0, CHUNK, QUARTER):
                first = max(q0 - 1, 0)
                worst_spans.append(b[first:first + 1, :] - b[q0 + QUARTER - 1:q0 + QUARTER, :])

    def hgrn_load(h, ch):
        rows = pl.ds(ch * CHUNK, CHUNK)
        return rows, qh_s[h, rows, :], kh_s[h, rows, :], vh_s[h, rows, :], b_s[h, rows, :], state_s[h]

    def hgrn_finish(h, rows, q, k, v, b, st, scores, diag):
        b_last = b[CHUNK - 1:CHUNK, :]
        q_in = (q * jnp.exp2(b)).astype(BF16)
        o = _dot(scores.astype(BF16), v) + _dot_nt(q_in, st.astype(BF16))
        if diag is not None:
            o = o + diag * v.astype(F32)
        k_out = (k * jnp.exp2(b_last - b)).astype(BF16)
        state_s[h] = st * jnp.exp2(b_last) + _dot_tn(v, k_out)
        ms = jnp.mean(o * o, axis=-1, keepdims=True)
        on = o * lax.rsqrt(ms + NORM_EPS) * ghead
        yin_s[h, rows, :] = (on * _silu(zh_s[h, rows, :])).astype(BF16)

    def hgrn_head_robust(h, ch):
        rows, q, k, v, b, st = hgrn_load(h, ch)
        lf = lf_s[h, rows, :]
        scores = jnp.zeros((CHUNK, CHUNK), F32)
        for level in range(VREG_LEVEL):
            xk = _level_operands(q, k, b, lf, level, row_c)[0].astype(BF16)
            scores = jnp.where(lvl == level, _dot_nt(xk, xk), scores)
        prods = {}
        for level in range(VREG_LEVEL, LEVELS):
            xk, xr = _level_operands(q, k, b, lf, level, row_c)
            prods[level] = _dot_nt(xr.astype(BF16), xk.astype(BF16))
        groups = []
        for g in range(CHUNK // SUBLANES):
            rs = slice(g * SUBLANES, (g + 1) * SUBLANES)
            acc = scores[rs]
            for level in range(VREG_LEVEL, LEVELS):
                if (g * SUBLANES) & (1 << level):
                    ri = _right_group_index(g, level)
                    acc = jnp.where(lvl[rs] == level, prods[level][ri * SUBLANES:(ri + 1) * SUBLANES], acc)
            groups.append(acc)
        scores = jnp.concatenate(groups, axis=0)
        hgrn_finish(h, rows, q, k, v, b, st, scores, jnp.sum(q * k, axis=-1, keepdims=True))

    def hgrn_fast_scores(h, ch):
        rows, q, k, v, b, st = hgrn_load(h, ch)
        mid_lo = jnp.broadcast_to(b[QUARTER - 1:QUARTER, :], (HALF, LANES))
        mid_hi = jnp.broadcast_to(b[HALF + QUARTER - 1:HALF + QUARTER, :], (HALF, LANES))
        d = b - jnp.concatenate([mid_lo, mid_hi], axis=0)
        q_own = (q * jnp.exp2(d)).astype(BF16)
        k_own = (k * jnp.exp2(-d)).astype(BF16)
        q_cross = (q[HALF:] * jnp.exp2(b[HALF:] - mid_lo)).astype(BF16)
        res = _dot_nt(jnp.concatenate([q_own, q_cross], axis=0), k_own)
        b_last = b[CHUNK - 1:CHUNK, :]
        return dict(rows=rows, res=res, v=v, st=st, q_in=(q * jnp.exp2(b)).astype(BF16),
                    k_out=(k * jnp.exp2(b_last - b)).astype(BF16), keep=jnp.exp2(b_last))

    def hgrn_fast_mix(h, t):
        res, v, st = t["res"], t["v"], t["st"]
        first_half_keys = lax.broadcasted_iota(jnp.int32, (HALF, LANES), 1) < HALF
        bottom = jnp.where(first_half_keys, res[CHUNK:], res[HALF:CHUNK])
        scores = jnp.where(causal, jnp.concatenate([res[:HALF], bottom], axis=0), 0.0)
        o = _dot(scores.astype(BF16), v) + _dot_nt(t["q_in"], st.astype(BF16))
        state_s[h] = st * t["keep"] + _dot_tn(v, t["k_out"])
        return o

    def hgrn_fast_gate(h, t, o):
        ms = jnp.mean(o * o, axis=-1, keepdims=True)
        on = o * lax.rsqrt(ms + NORM_EPS) * ghead
        yin_s[h, t["rows"], :] = (on * _silu(zh_s[h, t["rows"], :])).astype(BF16)

    def hgrn_heads_fast(heads, ch):
        parts = [hgrn_fast_scores(h, ch) for h in heads]
        outs = [hgrn_fast_mix(h, t) for h, t in zip(heads, parts)]
        for h, t, o in zip(heads, parts, outs):
            hgrn_fast_gate(h, t, o)

    def attn_scores(n, g, pos):
        qrows = slice(n * CHUNK, (n + 1) * CHUNK)
        krows = slice(n * CHUNK, (n + 2) * CHUNK)
        pairs = range(g * PAIRS_PER_KV, (g + 1) * PAIRS_PER_KV)
        q_stack = jnp.concatenate([qa_s[p, qrows, :] for p in pairs], axis=0)
        s_all = _dot_nt(q_stack, kpad_s[2 * g + pos, krows, :])
        for i, p in enumerate(pairs):
            head = 2 * p + pos
            s = s_all[i * CHUNK:(i + 1) * CHUNK]
            s_prev = jnp.where(jnp.logical_or(j > 0, n > 0), s[:, :CHUNK], NEG_INF)
            sc = jnp.where(causal, s[:, CHUNK:], s_prev)
            sc_s[n % 2 * ATT_HEADS + head] = sc
            m = jnp.maximum(jnp.max(sc, axis=-1, keepdims=True), sink_ref[head])
            m_s[n % 2 * ATT_HEADS + head] = jnp.broadcast_to(m, (CHUNK, LANES))

    def attn_out(n, g):
        qrows = slice(n * CHUNK, (n + 1) * CHUNK)
        krows = slice(n * CHUNK, (n + 2) * CHUNK)
        pairs = range(g * PAIRS_PER_KV, (g + 1) * PAIRS_PER_KV)
        res = []
        for pos in range(2):
            for i, p in enumerate(pairs):
                head = 2 * p + pos
                pr = jnp.exp(sc_s[n % 2 * ATT_HEADS + head] - m_s[n % 2 * ATT_HEADS + head])
                pcat = jnp.concatenate([jnp.where(causal, 0.0, pr), jnp.where(causal, pr, 0.0)], axis=1)
                pst_s[pos, i * CHUNK:(i + 1) * CHUNK, :] = pcat.astype(BF16)
            vp = jnp.concatenate([vpad_s[2 * g + pos, krows, :], ones_blk], axis=1)
            res.append(_dot(pst_s[pos], vp))
        for i, p in enumerate(pairs):
            out = None
            for pos in range(2):
                head = 2 * p + pos
                r = res[pos][i * CHUNK:(i + 1) * CHUNK]
                denom = r[:, LANES:] + jnp.exp(sink_ref[head] - m_s[n % 2 * ATT_HEADS + head])
                o = r[:, :LANES] / denom
                out = o if out is None else out + o
            yin_s[HG_HEADS + p, qrows, :] = (out * _silu(za_s[p, qrows, :])).astype(BF16)

    def out_proj_hgrn(nb):
        cols = slice(nb * COL_BLOCK, (nb + 1) * COL_BLOCK)
        yin = jnp.concatenate([yin_s[i] for i in range(HG_HEADS)], axis=1)
        y_s[:, cols] = _dot(yin, wout_ref[0:HG_WIDTH, cols])

    def out_proj_attn(nb):
        cols = slice(nb * COL_BLOCK, (nb + 1) * COL_BLOCK)
        yin = jnp.concatenate([yin_s[HG_HEADS + i] for i in range(ATT_PAIRS)], axis=1)
        y_s[:, cols] = y_s[:, cols] + _dot(yin, wout_ref[HG_WIDTH:MIX_WIDTH, cols])

    def post_norm():
        for rb in range(T // ROW_BLOCK):
            rows = slice(rb * ROW_BLOCK, (rb + 1) * ROW_BLOCK)
            y = y_s[rows, :]
            ms = jnp.mean(y * y, axis=-1, keepdims=True)
            o_ref[0, rows, :] = x_ref[0, rows, :] + y * lax.rsqrt(ms + NORM_EPS) * gpost_ref[...]

    n_chunks = T // CHUNK
    n_pairs = HG_HEADS // 2
    assert ATT_WIDTH // COL_BLOCK == 2 * KV_HEADS and (n_chunks * KV_HEADS) % (D_MODEL // COL_BLOCK) == 0
    pre_norm()
    stbk_s[...] = state_s[...]
    proj_items = []
    for c in range(n_pairs):
        proj_items += [lambda c=c: proj_q(c), lambda c=c: proj_f(c), lambda c=c: proj_i(c),
                       lambda c=c: (proj_z(c), cumsum_pair(c))]
    proj_items += [proj_kv] + [lambda a=a: proj_qa(a) for a in range(ATT_WIDTH // COL_BLOCK)]
    mixer_at = {}
    for grp in range(HG_HEADS // HEAD_GROUP):
        heads = tuple(range(grp * HEAD_GROUP, (grp + 1) * HEAD_GROUP))
        ready = 4 * (HEAD_GROUP // 2) * (grp + 1) - 1
        for ch in range(n_chunks):
            mixer_at[ready + MIX_SPACING * (ch + 1)] = (heads, ch)
    assert max(mixer_at) < len(proj_items)
    for i, item in enumerate(proj_items):
        item()
        if i in mixer_at:
            hgrn_heads_fast(*mixer_at[i])
    attn_units = [(g, pos) for g in range(KV_HEADS) for pos in range(2)]
    for a in range(ATT_WIDTH // COL_BLOCK):
        proj_za(a)
        attn_scores(0, *attn_units[a])
    out_every = n_chunks * KV_HEADS // (D_MODEL // COL_BLOCK)
    for n in range(n_chunks):
        for g in range(KV_HEADS):
            attn_out(n, g)
            if n + 1 < n_chunks:
                attn_scores(n + 1, g, 0)
                attn_scores(n + 1, g, 1)
            item = n * KV_HEADS + g
            if item % out_every == out_every - 1:
                out_proj_hgrn(item // out_every)

    worst = worst_spans[0]
    for span in worst_spans[1:]:
        worst = jnp.maximum(worst, span)
    fast_ok = jnp.max(worst) <= FAST_SPAN_LIMIT

    @pl.when(jnp.logical_not(fast_ok))
    def _():
        state_s[...] = stbk_s[...]
        for ch in range(n_chunks):
            def head_body(h, carry, ch=ch):
                hgrn_head_robust(h, ch)
                return carry
            lax.fori_loop(0, HG_HEADS, head_body, 0, unroll=ROBUST_HEAD_UNROLL)
        for nb in range(D_MODEL // COL_BLOCK):
            out_proj_hgrn(nb)

    kpad_s[:, 0:CHUNK, :] = kpad_s[:, T:T + CHUNK, :]
    vpad_s[:, 0:CHUNK, :] = vpad_s[:, T:T + CHUNK, :]

    for nb in range(D_MODEL // COL_BLOCK):
        out_proj_attn(nb)
    post_norm()


def _level_matrix():
    t = np.arange(CHUNK)[:, None]
    s = np.arange(CHUNK)[None, :]
    x = np.bitwise_xor(t, s)
    msb = np.where(x > 0, np.floor(np.log2(np.maximum(x, 1))), -1).astype(np.int32)
    return np.where(s < t, msb, -1).astype(np.int32)


def _rope_tables(seq):
    half = ATT_DIM // 2
    inv_freq = ROPE_THETA ** (-jnp.arange(half, dtype=F32) / half)
    ang = jnp.arange(seq, dtype=F32)[:, None] * inv_freq[None, :]
    cos = jnp.tile(jnp.cos(ang), (1, LANES // half))
    sin = jnp.sin(ang)
    sin = jnp.tile(jnp.concatenate([-sin, sin], axis=1), (1, LANES // ATT_DIM))
    return cos, sin


def _layer(x, layer, w_in, w_out, g_pre, g_post, lb, g_head, sinks, tables):
    B, S, _ = x.shape
    T = SEQ_TILE
    cosq, sinq, cosk, sink_rot, lvl, tri = tables
    lbf = jnp.maximum(lb, LB_FLOOR)

    def const(shape):
        return pl.BlockSpec(shape, lambda b, j: (0,) * len(shape), pipeline_mode=pl.Buffered(1))

    def layer_slab(shape):
        return pl.BlockSpec((None,) + shape, lambda b, j: (layer,) + (0,) * len(shape), pipeline_mode=pl.Buffered(1))

    def seq_table():
        return pl.BlockSpec((T, LANES), lambda b, j: (j, 0))

    in_specs = [
        pl.BlockSpec((1, T, D_MODEL), lambda b, j: (b, j, 0)),
        layer_slab((D_MODEL, IN_WIDTH)),
        layer_slab((MIX_WIDTH, D_MODEL)),
        const((1, D_MODEL)),
        const((1, D_MODEL)),
        const((1, HG_WIDTH)),
        const((1, HG_WIDTH)),
        const((1, HG_WIDTH)),
        const((1, HG_DIM)),
        pl.BlockSpec(memory_space=pltpu.SMEM),
        seq_table(), seq_table(), seq_table(), seq_table(),
        const((CHUNK, CHUNK)),
        const((CHUNK, 2 * CHUNK)),
    ]
    scratch = [
        pltpu.VMEM((T, D_MODEL), BF16),
        pltpu.VMEM((HG_HEADS, T, LANES), F32),
        pltpu.VMEM((HG_HEADS, T, LANES), F32),
        pltpu.VMEM((HG_HEADS, T, LANES), BF16),
        pltpu.VMEM((HG_HEADS, T, LANES), F32),
        pltpu.VMEM((HG_HEADS, T, LANES), F32),
        pltpu.VMEM((HG_HEADS, T, LANES), F32),
        pltpu.VMEM((ATT_PAIRS, T, LANES), BF16),
        pltpu.VMEM((2 * KV_HEADS, CHUNK + T, LANES), BF16),
        pltpu.VMEM((2 * KV_HEADS, CHUNK + T, LANES), BF16),
        pltpu.VMEM((ATT_PAIRS, T, LANES), F32),
        pltpu.VMEM((MIX_WIDTH // LANES, T, LANES), BF16),
        pltpu.VMEM((HG_HEADS, HG_DIM, HG_DIM), F32),
        pltpu.VMEM((2 * ATT_HEADS, CHUNK, LANES), F32),
        pltpu.VMEM((2 * ATT_HEADS, CHUNK, LANES), F32),
        pltpu.VMEM((HG_HEADS, HG_DIM, HG_DIM), F32),
        pltpu.VMEM((T, D_MODEL), F32),
        pltpu.VMEM((2, PAIRS_PER_KV * CHUNK, 2 * CHUNK), BF16),
    ]
    return pl.pallas_call(
        _layer_kernel,
        out_shape=jax.ShapeDtypeStruct(x.shape, x.dtype),
        grid=(B, S // T),
        in_specs=in_specs,
        out_specs=pl.BlockSpec((1, T, D_MODEL), lambda b, j: (b, j, 0)),
        scratch_shapes=scratch,
        compiler_params=pltpu.CompilerParams(
            dimension_semantics=("arbitrary", "arbitrary"),
            vmem_limit_bytes=VMEM_LIMIT_BYTES),
        name="hybrid_layer",
    )(x, w_in, w_out, g_pre[None, :], g_post[None, :],
      lbf[None, :], (1.0 - lb)[None, :], (lbf - lb)[None, :], g_head[None, :], sinks,
      cosq, sinq, cosk, sink_rot, lvl, tri)


def kernel(x, w_in, w_out, g_pre, g_post, lb_param, g_head, sinks):
    depth = w_in.shape[0]
    seq = x.shape[1]
    p = jax.nn.softmax(lb_param.astype(F32), axis=0)
    lower_bounds = jnp.cumsum(p, axis=0) - p[0:1]
    cos, sin = _rope_tables(seq)
    tri = jnp.asarray(np.tile(np.tril(np.ones((CHUNK, CHUNK), np.float32)), (1, 2)), BF16)
    tables = (cos * ATT_SCALE, sin * ATT_SCALE, cos, sin, jnp.asarray(_level_matrix()), tri)
    w_in, w_out = w_in.astype(BF16), w_out.astype(BF16)
    for l in range(depth):
        x = _layer(x, l, w_in, w_out, g_pre[l], g_post[l], lower_bounds[l], g_head[l], sinks[l], tables)
    return x
```

```python
import math

import numpy as np
import jax
import jax.numpy as jnp
from jax import lax
from jax.experimental import pallas as pl
from jax.experimental.pallas import tpu as pltpu

D_MODEL = 1024
HG_WIDTH = 1024
HG_HEADS = 8
HG_DIM = 128
ATT_WIDTH = 1024
ATT_HEADS = 16
ATT_DIM = 64
ATT_PAIRS = ATT_HEADS // 2
KV_HEADS = 2
PAIRS_PER_KV = ATT_PAIRS // KV_HEADS
IN_WIDTH = 6400
MIX_WIDTH = 2048
ATT_SCALE = 1.0 / math.sqrt(ATT_DIM)
ROPE_THETA = 10000.0
NORM_EPS = 1e-6
NEG_INF = -1e30
LB_FLOOR = 1e-20

LANES = 128
SUBLANES = 8
CHUNK = 128
HALF = CHUNK // 2
QUARTER = CHUNK // 4
FAST_SPAN_LIMIT = 64.0
LEVELS = 7
VREG_LEVEL = 3
SEQ_TILE = 256
COL_BLOCK = 256
ROW_BLOCK = 64
ROBUST_HEAD_UNROLL = 4
HEAD_GROUP = 4
MIX_SPACING = 1
BLOCKS_PER_PAIR = 4
VMEM_LIMIT_BYTES = 48 * 1024 * 1024

OFF_QH, OFF_F, OFF_I, OFF_ZH, OFF_QA, OFF_KV, OFF_ZA = 0, 1024, 2048, 3072, 4096, 5120, 5376

F32 = jnp.float32
BF16 = jnp.bfloat16


def _dot(a, b):
    return jnp.dot(a, b, preferred_element_type=F32)


def _dot_nt(a, b):
    return lax.dot_general(a, b, (((1,), (1,)), ((), ())), preferred_element_type=F32)


def _dot_tn(a, b):
    return lax.dot_general(a, b, (((0,), (0,)), ((), ())), preferred_element_type=F32)


def _silu(x):
    return x * jax.nn.sigmoid(x)


def _rope(x, cos, sin, lo_half):
    swapped = jnp.where(lo_half, pltpu.roll(x, LANES - ATT_DIM // 2, 1), pltpu.roll(x, ATT_DIM // 2, 1))
    return x * cos + swapped * sin


def _hold_rows(b, period, offset):
    pieces = []
    for blk in range(CHUNK // period):
        row = blk * period + offset
        pieces.append(jnp.broadcast_to(b[row:row + 1, :], (period, LANES)))
    return pieces[0] if len(pieces) == 1 else jnp.concatenate(pieces, axis=0)


def _level_operands(q, k, b, lf, level, row_idx):
    half = 1 << level
    if level >= VREG_LEVEL:
        left, right = [], []
        for blk in range(CHUNK // (2 * half)):
            lo, mid, hi = blk * 2 * half, blk * 2 * half + half, (blk + 1) * 2 * half
            r = jnp.broadcast_to(b[mid - 1:mid, :], (half, LANES))
            left.append(k[lo:mid] * jnp.exp2(r - b[lo:mid]))
            right.append(q[mid:hi] * jnp.exp2(b[mid:hi] - r))
        both = [piece for pair in zip(left, right) for piece in pair]
        return jnp.concatenate(both, axis=0), (right[0] if len(right) == 1 else jnp.concatenate(right, axis=0))
    if 2 * half == SUBLANES:
        arg = -jnp.abs(b - _hold_rows(b, SUBLANES, half - 1))
    elif 4 * half == SUBLANES:
        upper = (row_idx & (SUBLANES - 1)) >= 2 * half
        arg = -jnp.abs(b - jnp.where(upper, _hold_rows(b, SUBLANES, 3 * half - 1), _hold_rows(b, SUBLANES, half - 1)))
    else:
        arg = jnp.where((row_idx & 1) == 1, lf, 0.0)
    is_right = (row_idx & (2 * half - 1)) >= half
    return jnp.where(is_right, q, k) * jnp.exp2(arg), None


def _right_group_index(group, level):
    half_groups = (1 << level) // SUBLANES
    return (group // (2 * half_groups)) * half_groups + group % half_groups


def _layer_kernel(x_ref, win_ref, wout_ref, gpre_ref, gpost_ref, lbf_ref, oml_ref, lbd_ref, ghead_ref,
                  sink_ref, cosq_ref, sinq_ref, cosk_ref, sink_rot_ref, lvl_ref, tri_ref,
                  o_ref,
                  hn_s, qh_s, kh_s, vh_s, lf_s, b_s, zh_s, qa_s, kpad_s, vpad_s, za_s, yin_s, state_s,
                  sc_s, m_s, stbk_s, y_s, pst_s):
    T = SEQ_TILE
    j = pl.program_id(1)

    @pl.when(j == 0)
    def _():
        state_s[...] = jnp.zeros_like(state_s)
        kpad_s[:, 0:CHUNK, :] = jnp.zeros((2 * KV_HEADS, CHUNK, LANES), BF16)
        vpad_s[:, 0:CHUNK, :] = jnp.zeros((2 * KV_HEADS, CHUNK, LANES), BF16)

    lane = lax.broadcasted_iota(jnp.int32, (T, LANES), 1)
    lo_half = (lane & (ATT_DIM - 1)) < ATT_DIM // 2
    lo_head = lane < ATT_DIM
    row_c = lax.broadcasted_iota(jnp.int32, (CHUNK, LANES), 0)
    col_c = lax.broadcasted_iota(jnp.int32, (CHUNK, LANES), 1)
    causal = row_c >= col_c
    lvl = lvl_ref[...]
    ghead = ghead_ref[...]
    ones_blk = jnp.ones((2 * CHUNK, LANES), BF16)
    worst_spans = []

    def pre_norm():
        for rb in range(T // ROW_BLOCK):
            rows = slice(rb * ROW_BLOCK, (rb + 1) * ROW_BLOCK)
            x = x_ref[0, rows, :]
            ms = jnp.mean(x * x, axis=-1, keepdims=True)
            hn_s[rows, :] = (x * lax.rsqrt(ms + NORM_EPS) * gpre_ref[...]).astype(BF16)

    def proj(col):
        return _dot(hn_s[...], win_ref[:, col:col + COL_BLOCK])

    def proj_q(c):
        qf = _silu(proj(OFF_QH + c * COL_BLOCK))
        qh_s[2 * c] = qf[:, :LANES]
        qh_s[2 * c + 1] = qf[:, LANES:]

    def proj_f(c):
        cols = slice(c * COL_BLOCK, (c + 1) * COL_BLOCK)
        sig = jax.nn.sigmoid(proj(OFF_F + c * COL_BLOCK))
        oml = oml_ref[:, cols]
        lf = jnp.log2(lbf_ref[:, cols] + oml * sig)
        k = oml * (1.0 - sig) - lbd_ref[:, cols]
        lf_s[2 * c] = lf[:, :LANES]
        lf_s[2 * c + 1] = lf[:, LANES:]
        kh_s[2 * c] = k[:, :LANES]
        kh_s[2 * c + 1] = k[:, LANES:]

    def proj_i(c):
        res = proj(OFF_I + c * COL_BLOCK).astype(BF16)
        vh_s[2 * c] = res[:, :LANES]
        vh_s[2 * c + 1] = res[:, LANES:]

    def proj_z(c):
        res = proj(OFF_ZH + c * COL_BLOCK)
        zh_s[2 * c] = res[:, :LANES]
        zh_s[2 * c + 1] = res[:, LANES:]

    def proj_qa(c):
        res = proj(OFF_QA + c * COL_BLOCK)
        qa_s[2 * c] = _rope(res[:, :LANES], cosq_ref[...], sinq_ref[...], lo_half).astype(BF16)
        qa_s[2 * c + 1] = _rope(res[:, LANES:], cosq_ref[...], sinq_ref[...], lo_half).astype(BF16)

    def proj_kv():
        res = proj(OFF_KV)
        kr = _rope(res[:, :LANES], cosk_ref[...], sink_rot_ref[...], lo_half)
        kr_sw = pltpu.roll(kr, ATT_DIM, 1)
        va = res[:, LANES:]
        va_sw = pltpu.roll(va, ATT_DIM, 1)
        cur = slice(CHUNK, CHUNK + T)
        zero = jnp.zeros((T, LANES), F32)
        kpad_s[0, cur, :] = jnp.where(lo_head, kr, zero).astype(BF16)
        kpad_s[1, cur, :] = jnp.where(lo_head, zero, kr_sw).astype(BF16)
        kpad_s[2, cur, :] = jnp.where(lo_head, kr_sw, zero).astype(BF16)
        kpad_s[3, cur, :] = jnp.where(lo_head, zero, kr).astype(BF16)
        vpad_s[0, cur, :] = jnp.where(lo_head, va, zero).astype(BF16)
        vpad_s[1, cur, :] = jnp.where(lo_head, zero, va_sw).astype(BF16)
        vpad_s[2, cur, :] = jnp.where(lo_head, va_sw, zero).astype(BF16)
        vpad_s[3, cur, :] = jnp.where(lo_head, zero, va).astype(BF16)

    def proj_za(c):
        res = proj(OFF_ZA + c * COL_BLOCK)
        za_s[2 * c] = res[:, :LANES]
        za_s[2 * c + 1] = res[:, LANES:]

    def cumsum_pair(c):
        for ch in range(T // CHUNK):
            rows = slice(ch * CHUNK, (ch + 1) * CHUNK)
            lf = jnp.concatenate([lf_s[2 * c, rows, :], lf_s[2 * c + 1, rows, :]], axis=1)
            hi = lf.astype(BF16)
            lo = (lf - hi.astype(F32)).astype(BF16)
            b = _dot(tri_ref[...], jnp.concatenate([hi, lo], axis=0))
            b_s[2 * c, rows, :] = b[:, :LANES]
            b_s[2 * c + 1, rows, :] = b[:, LANES:]
            for q0 in range(0, CHUNK, QUARTER):
                first = max(q0 - 1, 0)
                worst_spans.append(b[first:first + 1, :] - b[q0 + QUARTER - 1:q0 + QUARTER, :])

    def hgrn_load(h, ch):
        rows = pl.ds(ch * CHUNK, CHUNK)
        return rows, qh_s[h, rows, :], kh_s[h, rows, :], vh_s[h, rows, :], b_s[h, rows, :], state_s[h]

    def hgrn_finish(h, rows, q, k, v, b, st, scores, diag):
        b_last = b[CHUNK - 1:CHUNK, :]
        q_in = (q * jnp.exp2(b)).astype(BF16)
        o = _dot(scores.astype(BF16), v) + _dot_nt(q_in, st.astype(BF16))
        if diag is not None:
            o = o + diag * v.astype(F32)
        k_out = (k * jnp.exp2(b_last - b)).astype(BF16)
        state_s[h] = st * jnp.exp2(b_last) + _dot_tn(v, k_out)
        ms = jnp.mean(o * o, axis=-1, keepdims=True)
        on = o * lax.rsqrt(ms + NORM_EPS) * ghead
        yin_s[h, rows, :] = (on * _silu(zh_s[h, rows, :])).astype(BF16)

    def hgrn_head_robust(h, ch):
        rows, q, k, v, b, st = hgrn_load(h, ch)
        lf = lf_s[h, rows, :]
        scores = jnp.zeros((CHUNK, CHUNK), F32)
        for level in range(VREG_LEVEL):
            xk = _level_operands(q, k, b, lf, level, row_c)[0].astype(BF16)
            scores = jnp.where(lvl == level, _dot_nt(xk, xk), scores)
        prods = {}
        for level in range(VREG_LEVEL, LEVELS):
            xk, xr = _level_operands(q, k, b, lf, level, row_c)
            prods[level] = _dot_nt(xr.astype(BF16), xk.astype(BF16))
        groups = []
        for g in range(CHUNK // SUBLANES):
            rs = slice(g * SUBLANES, (g + 1) * SUBLANES)
            acc = scores[rs]
            for level in range(VREG_LEVEL, LEVELS):
                if (g * SUBLANES) & (1 << level):
                    ri = _right_group_index(g, level)
                    acc = jnp.where(lvl[rs] == level, prods[level][ri * SUBLANES:(ri + 1) * SUBLANES], acc)
            groups.append(acc)
        scores = jnp.concatenate(groups, axis=0)
        hgrn_finish(h, rows, q, k, v, b, st, scores, jnp.sum(q * k, axis=-1, keepdims=True))

    def hgrn_fast_scores(h, ch):
        rows, q, k, v, b, st = hgrn_load(h, ch)
        mid_lo = jnp.broadcast_to(b[QUARTER - 1:QUARTER, :], (HALF, LANES))
        mid_hi = jnp.broadcast_to(b[HALF + QUARTER - 1:HALF + QUARTER, :], (HALF, LANES))
        d = b - jnp.concatenate([mid_lo, mid_hi], axis=0)
        q_own = (q * jnp.exp2(d)).astype(BF16)
        k_own = (k * jnp.exp2(-d)).astype(BF16)
        q_cross = (q[HALF:] * jnp.exp2(b[HALF:] - mid_lo)).astype(BF16)
        res = _dot_nt(jnp.concatenate([q_own, q_cross], axis=0), k_own)
        b_last = b[CHUNK - 1:CHUNK, :]
        return dict(rows=rows, res=res, v=v, st=st, q_in=(q * jnp.exp2(b)).astype(BF16),
                    k_out=(k * jnp.exp2(b_last - b)).astype(BF16), keep=jnp.exp2(b_last))

    def hgrn_fast_mix(h, t):
        res, v, st = t["res"], t["v"], t["st"]
        first_half_keys = lax.broadcasted_iota(jnp.int32, (HALF, LANES), 1) < HALF
        bottom = jnp.where(first_half_keys, res[CHUNK:], res[HALF:CHUNK])
        scores = jnp.where(causal, jnp.concatenate([res[:HALF], bottom], axis=0), 0.0)
        o = _dot(scores.astype(BF16), v) + _dot_nt(t["q_in"], st.astype(BF16))
        state_s[h] = st * t["keep"] + _dot_tn(v, t["k_out"])
        return o

    def hgrn_fast_gate(h, t, o):
        ms = jnp.mean(o * o, axis=-1, keepdims=True)
        on = o * lax.rsqrt(ms + NORM_EPS) * ghead
        yin_s[h, t["rows"], :] = (on * _silu(zh_s[h, t["rows"], :])).astype(BF16)

    def hgrn_heads_fast(heads, ch):
        parts = [hgrn_fast_scores(h, ch) for h in heads]
        outs = [hgrn_fast_mix(h, t) for h, t in zip(heads, parts)]
        for h, t, o in zip(heads, parts, outs):
            hgrn_fast_gate(h, t, o)

    def attn_scores(n, g, pos):
        qrows = slice(n * CHUNK, (n + 1) * CHUNK)
        krows = slice(n * CHUNK, (n + 2) * CHUNK)
        pairs = range(g * PAIRS_PER_KV, (g + 1) * PAIRS_PER_KV)
        q_stack = jnp.concatenate([qa_s[p, qrows, :] for p in pairs], axis=0)
        s_all = _dot_nt(q_stack, kpad_s[2 * g + pos, krows, :])
        for i, p in enumerate(pairs):
            head = 2 * p + pos
            s = s_all[i * CHUNK:(i + 1) * CHUNK]
            s_prev = jnp.where(jnp.logical_or(j > 0, n > 0), s[:, :CHUNK], NEG_INF)
            sc = jnp.where(causal, s[:, CHUNK:], s_prev)
            sc_s[n % 2 * ATT_HEADS + head] = sc
            m = jnp.maximum(jnp.max(sc, axis=-1, keepdims=True), sink_ref[head])
            m_s[n % 2 * ATT_HEADS + head] = jnp.broadcast_to(m, (CHUNK, LANES))

    def attn_out(n, g):
        qrows = slice(n * CHUNK, (n + 1) * CHUNK)
        krows = slice(n * CHUNK, (n + 2) * CHUNK)
        pairs = range(g * PAIRS_PER_KV, (g + 1) * PAIRS_PER_KV)
        res = []
        for pos in range(2):
            for i, p in enumerate(pairs):
                head = 2 * p + pos
                pr = jnp.exp(sc_s[n % 2 * ATT_HEADS + head] - m_s[n % 2 * ATT_HEADS + head])
                pcat = jnp.concatenate([jnp.where(causal, 0.0, pr), jnp.where(causal, pr, 0.0)], axis=1)
                pst_s[pos, i * CHUNK:(i + 1) * CHUNK, :] = pcat.astype(BF16)
            vp = jnp.concatenate([vpad_s[2 * g + pos, krows, :], ones_blk], axis=1)
            res.append(_dot(pst_s[pos], vp))
        for i, p in enumerate(pairs):
            out = None
            for pos in range(2):
                head = 2 * p + pos
                r = res[pos][i * CHUNK:(i + 1) * CHUNK]
                denom = r[:, LANES:] + jnp.exp(sink_ref[head] - m_s[n % 2 * ATT_HEADS + head])
                o = r[:, :LANES] / denom
                out = o if out is None else out + o
            yin_s[HG_HEADS + p, qrows, :] = (out * _silu(za_s[p, qrows, :])).astype(BF16)

    def out_proj_hgrn(nb):
        cols = slice(nb * COL_BLOCK, (nb + 1) * COL_BLOCK)
        yin = jnp.concatenate([yin_s[i] for i in range(HG_HEADS)], axis=1)
        y_s[:, cols] = _dot(yin, wout_ref[0:HG_WIDTH, cols])

    def out_proj_attn(nb):
        cols = slice(nb * COL_BLOCK, (nb + 1) * COL_BLOCK)
        yin = jnp.concatenate([yin_s[HG_HEADS + i] for i in range(ATT_PAIRS)], axis=1)
        y_s[:, cols] = y_s[:, cols] + _dot(yin, wout_ref[HG_WIDTH:MIX_WIDTH, cols])

    def post_norm():
        for rb in range(T // ROW_BLOCK):
            rows = slice(rb * ROW_BLOCK, (rb + 1) * ROW_BLOCK)
            y = y_s[rows, :]
            ms = jnp.mean(y * y, axis=-1, keepdims=True)
            o_ref[0, rows, :] = x_ref[0, rows, :] + y * lax.rsqrt(ms + NORM_EPS) * gpost_ref[...]

    n_chunks = T // CHUNK
    n_pairs = HG_HEADS // 2
    assert ATT_WIDTH // COL_BLOCK == 2 * KV_HEADS and (n_chunks * KV_HEADS) % (D_MODEL // COL_BLOCK) == 0
    pre_norm()
    stbk_s[...] = state_s[...]
    proj_items = []
    for c in range(n_pairs):
        proj_items += [lambda c=c: proj_q(c), lambda c=c: proj_f(c), lambda c=c: proj_i(c),
                       lambda c=c: (proj_z(c), cumsum_pair(c))]
    proj_items += [proj_kv] + [lambda a=a: proj_qa(a) for a in range(ATT_WIDTH // COL_BLOCK)]
    mixer_at = {}
    for grp in range(HG_HEADS // HEAD_GROUP):
        heads = tuple(range(grp * HEAD_GROUP, (grp + 1) * HEAD_GROUP))
        ready = BLOCKS_PER_PAIR * (HEAD_GROUP // 2) * (grp + 1) - 1
        for ch in range(n_chunks):
            mixer_at[ready + MIX_SPACING * (ch + 1)] = (heads, ch)
    assert max(mixer_at) < len(proj_items)
    for i, item in enumerate(proj_items):
        item()
        if i in mixer_at:
            hgrn_heads_fast(*mixer_at[i])
    attn_units = [(g, pos) for g in range(KV_HEADS) for pos in range(2)]
    for a in range(ATT_WIDTH // COL_BLOCK):
        proj_za(a)
        attn_scores(0, *attn_units[a])
    out_every = n_chunks * KV_HEADS // (D_MODEL // COL_BLOCK)
    for n in range(n_chunks):
        for g in range(KV_HEADS):
            attn_out(n, g)
            if n + 1 < n_chunks:
                attn_scores(n + 1, g, 0)
                attn_scores(n + 1, g, 1)
            item = n * KV_HEADS + g
            if item % out_every == out_every - 1:
                out_proj_hgrn(item // out_every)

    worst = worst_spans[0]
    for span in worst_spans[1:]:
        worst = jnp.maximum(worst, span)
    fast_ok = jnp.max(worst) <= FAST_SPAN_LIMIT

    @pl.when(jnp.logical_not(fast_ok))
    def _():
        state_s[...] = stbk_s[...]
        for ch in range(n_chunks):
            def head_body(h, carry, ch=ch):
                hgrn_head_robust(h, ch)
                return carry
            lax.fori_loop(0, HG_HEADS, head_body, 0, unroll=ROBUST_HEAD_UNROLL)
        for nb in range(D_MODEL // COL_BLOCK):
            out_proj_hgrn(nb)

    kpad_s[:, 0:CHUNK, :] = kpad_s[:, T:T + CHUNK, :]
    vpad_s[:, 0:CHUNK, :] = vpad_s[:, T:T + CHUNK, :]

    for nb in range(D_MODEL // COL_BLOCK):
        out_proj_attn(nb)
    post_norm()


def _level_matrix():
    t = np.arange(CHUNK)[:, None]
    s = np.arange(CHUNK)[None, :]
    x = np.bitwise_xor(t, s)
    msb = np.where(x > 0, np.floor(np.log2(np.maximum(x, 1))), -1).astype(np.int32)
    return np.where(s < t, msb, -1).astype(np.int32)


def _rope_tables(seq):
    half = ATT_DIM // 2
    inv_freq = ROPE_THETA ** (-jnp.arange(half, dtype=F32) / half)
    ang = jnp.arange(seq, dtype=F32)[:, None] * inv_freq[None, :]
    cos = jnp.tile(jnp.cos(ang), (1, LANES // half))
    sin = jnp.sin(ang)
    sin = jnp.tile(jnp.concatenate([-sin, sin], axis=1), (1, LANES // ATT_DIM))
    return cos, sin


def _layer(x, layer, w_in, w_out, g_pre, g_post, lb, g_head, sinks, tables):
    B, S, _ = x.shape
    T = SEQ_TILE
    cosq, sinq, cosk, sink_rot, lvl, tri = tables
    lbf = jnp.maximum(lb, LB_FLOOR)

    def const(shape):
        return pl.BlockSpec(shape, lambda b, j: (0,) * len(shape), pipeline_mode=pl.Buffered(1))

    def layer_slab(shape):
        return pl.BlockSpec((None,) + shape, lambda b, j: (layer,) + (0,) * len(shape), pipeline_mode=pl.Buffered(1))

    def seq_table():
        return pl.BlockSpec((T, LANES), lambda b, j: (j, 0))

    in_specs = [
        pl.BlockSpec((1, T, D_MODEL), lambda b, j: (b, j, 0)),
        layer_slab((D_MODEL, IN_WIDTH)),
        layer_slab((MIX_WIDTH, D_MODEL)),
        const((1, D_MODEL)),
        const((1, D_MODEL)),
        const((1, HG_WIDTH)),
        const((1, HG_WIDTH)),
        const((1, HG_WIDTH)),
        const((1, HG_DIM)),
        pl.BlockSpec(memory_space=pltpu.SMEM),
        seq_table(), seq_table(), seq_table(), seq_table(),
        const((CHUNK, CHUNK)),
        const((CHUNK, 2 * CHUNK)),
    ]
    scratch = [
        pltpu.VMEM((T, D_MODEL), BF16),
        pltpu.VMEM((HG_HEADS, T, LANES), F32),
        pltpu.VMEM((HG_HEADS, T, LANES), F32),
        pltpu.VMEM((HG_HEADS, T, LANES), BF16),
        pltpu.VMEM((HG_HEADS, T, LANES), F32),
        pltpu.VMEM((HG_HEADS, T, LANES), F32),
        pltpu.VMEM((HG_HEADS, T, LANES), F32),
        pltpu.VMEM((ATT_PAIRS, T, LANES), BF16),
        pltpu.VMEM((2 * KV_HEADS, CHUNK + T, LANES), BF16),
        pltpu.VMEM((2 * KV_HEADS, CHUNK + T, LANES), BF16),
        pltpu.VMEM((ATT_PAIRS, T, LANES), F32),
        pltpu.VMEM((MIX_WIDTH // LANES, T, LANES), BF16),
        pltpu.VMEM((HG_HEADS, HG_DIM, HG_DIM), F32),
        pltpu.VMEM((2 * ATT_HEADS, CHUNK, LANES), F32),
        pltpu.VMEM((2 * ATT_HEADS, CHUNK, LANES), F32),
        pltpu.VMEM((HG_HEADS, HG_DIM, HG_DIM), F32),
        pltpu.VMEM((T, D_MODEL), F32),
        pltpu.VMEM((2, PAIRS_PER_KV * CHUNK, 2 * CHUNK), BF16),
    ]
    return pl.pallas_call(
        _layer_kernel,
        out_shape=jax.ShapeDtypeStruct(x.shape, x.dtype),
        grid=(B, S // T),
        in_specs=in_specs,
        out_specs=pl.BlockSpec((1, T, D_MODEL), lambda b, j: (b, j, 0)),
        scratch_shapes=scratch,
        compiler_params=pltpu.CompilerParams(
            dimension_semantics=("arbitrary", "arbitrary"),
            vmem_limit_bytes=VMEM_LIMIT_BYTES),
        name="hybrid_layer",
    )(x, w_in, w_out, g_pre[None, :], g_post[None, :],
      lbf[None, :], (1.0 - lb)[None, :], (lbf - lb)[None, :], g_head[None, :], sinks,
      cosq, sinq, cosk, sink_rot, lvl, tri)


def kernel(x, w_in, w_out, g_pre, g_post, lb_param, g_head, sinks):
    depth = w_in.shape[0]
    seq = x.shape[1]
    p = jax.nn.softmax(lb_param.astype(F32), axis=0)
    lower_bounds = jnp.cumsum(p, axis=0) - p[0:1]
    cos, sin = _rope_tables(seq)
    tri = jnp.asarray(np.tile(np.tril(np.ones((CHUNK, CHUNK), np.float32)), (1, 2)), BF16)
    tables = (cos * ATT_SCALE, sin * ATT_SCALE, cos, sin, jnp.asarray(_level_matrix()), tri)
    w_in, w_out = w_in.astype(BF16), w_out.astype(BF16)
    for l in range(depth):
        x = _layer(x, l, w_in, w_out, g_pre[l], g_post[l], lower_bounds[l], g_head[l], sinks[l], tables)
    return x
```

```python
import math

import numpy as np
import jax
import jax.numpy as jnp
from jax import lax
from jax.experimental import pallas as pl
from jax.experimental.pallas import tpu as pltpu

D_MODEL = 1024
HG_WIDTH = 1024
HG_HEADS = 8
HG_DIM = 128
ATT_WIDTH = 1024
ATT_HEADS = 16
ATT_DIM = 64
ATT_PAIRS = ATT_HEADS // 2
KV_HEADS = 2
PAIRS_PER_KV = ATT_PAIRS // KV_HEADS
IN_WIDTH = 6400
MIX_WIDTH = 2048
ATT_SCALE = 1.0 / math.sqrt(ATT_DIM)
ROPE_THETA = 10000.0
NORM_EPS = 1e-6
NEG_INF = -1e30
LB_FLOOR = 1e-20

LANES = 128
SUBLANES = 8
CHUNK = 128
HALF = CHUNK // 2
QUARTER = CHUNK // 4
FAST_SPAN_LIMIT = 64.0
LEVELS = 7
VREG_LEVEL = 3
SEQ_TILE = 256
COL_BLOCK = 256
ROW_BLOCK = 64
ROBUST_HEAD_UNROLL = 4
HEAD_GROUP = 4
MIX_SPACING = 1
GROUP_STAGGER = 1
BLOCKS_PER_PAIR = 4
VMEM_LIMIT_BYTES = 48 * 1024 * 1024

OFF_QH, OFF_F, OFF_I, OFF_ZH, OFF_QA, OFF_KV, OFF_ZA = 0, 1024, 2048, 3072, 4096, 5120, 5376

F32 = jnp.float32
BF16 = jnp.bfloat16


def _dot(a, b):
    return jnp.dot(a, b, preferred_element_type=F32)


def _dot_nt(a, b):
    return lax.dot_general(a, b, (((1,), (1,)), ((), ())), preferred_element_type=F32)


def _dot_tn(a, b):
    return lax.dot_general(a, b, (((0,), (0,)), ((), ())), preferred_element_type=F32)


def _silu(x):
    return x * jax.nn.sigmoid(x)


def _rope(x, cos, sin, lo_half):
    swapped = jnp.where(lo_half, pltpu.roll(x, LANES - ATT_DIM // 2, 1), pltpu.roll(x, ATT_DIM // 2, 1))
    return x * cos + swapped * sin


def _hold_rows(b, period, offset):
    pieces = []
    for blk in range(CHUNK // period):
        row = blk * period + offset
        pieces.append(jnp.broadcast_to(b[row:row + 1, :], (period, LANES)))
    return pieces[0] if len(pieces) == 1 else jnp.concatenate(pieces, axis=0)


def _level_operands(q, k, b, lf, level, row_idx):
    half = 1 << level
    if level >= VREG_LEVEL:
        left, right = [], []
        for blk in range(CHUNK // (2 * half)):
            lo, mid, hi = blk * 2 * half, blk * 2 * half + half, (blk + 1) * 2 * half
            r = jnp.broadcast_to(b[mid - 1:mid, :], (half, LANES))
            left.append(k[lo:mid] * jnp.exp2(r - b[lo:mid]))
            right.append(q[mid:hi] * jnp.exp2(b[mid:hi] - r))
        both = [piece for pair in zip(left, right) for piece in pair]
        return jnp.concatenate(both, axis=0), (right[0] if len(right) == 1 else jnp.concatenate(right, axis=0))
    if 2 * half == SUBLANES:
        arg = -jnp.abs(b - _hold_rows(b, SUBLANES, half - 1))
    elif 4 * half == SUBLANES:
        upper = (row_idx & (SUBLANES - 1)) >= 2 * half
        arg = -jnp.abs(b - jnp.where(upper, _hold_rows(b, SUBLANES, 3 * half - 1), _hold_rows(b, SUBLANES, half - 1)))
    else:
        arg = jnp.where((row_idx & 1) == 1, lf, 0.0)
    is_right = (row_idx & (2 * half - 1)) >= half
    return jnp.where(is_right, q, k) * jnp.exp2(arg), None


def _right_group_index(group, level):
    half_groups = (1 << level) // SUBLANES
    return (group // (2 * half_groups)) * half_groups + group % half_groups


def _layer_kernel(x_ref, win_ref, wout_ref, gpre_ref, gpost_ref, lbf_ref, oml_ref, lbd_ref, ghead_ref,
                  sink_ref, cosq_ref, sinq_ref, cosk_ref, sink_rot_ref, lvl_ref, tri_ref,
                  o_ref,
                  hn_s, qh_s, kh_s, vh_s, lf_s, b_s, zh_s, qa_s, kpad_s, vpad_s, za_s, yin_s, state_s,
                  sc_s, m_s, stbk_s, y_s, pst_s):
    T = SEQ_TILE
    j = pl.program_id(1)

    @pl.when(j == 0)
    def _():
        state_s[...] = jnp.zeros_like(state_s)
        kpad_s[:, 0:CHUNK, :] = jnp.zeros((2 * KV_HEADS, CHUNK, LANES), BF16)
        vpad_s[:, 0:CHUNK, :] = jnp.zeros((2 * KV_HEADS, CHUNK, LANES), BF16)

    lane = lax.broadcasted_iota(jnp.int32, (T, LANES), 1)
    lo_half = (lane & (ATT_DIM - 1)) < ATT_DIM // 2
    lo_head = lane < ATT_DIM
    row_c = lax.broadcasted_iota(jnp.int32, (CHUNK, LANES), 0)
    col_c = lax.broadcasted_iota(jnp.int32, (CHUNK, LANES), 1)
    causal = row_c >= col_c
    lvl = lvl_ref[...]
    ghead = ghead_ref[...]
    ones_blk = jnp.ones((2 * CHUNK, LANES), BF16)
    worst_spans = []

    def pre_norm():
        for rb in range(T // ROW_BLOCK):
            rows = slice(rb * ROW_BLOCK, (rb + 1) * ROW_BLOCK)
            x = x_ref[0, rows, :]
            ms = jnp.mean(x * x, axis=-1, keepdims=True)
            hn_s[rows, :] = (x * lax.rsqrt(ms + NORM_EPS) * gpre_ref[...]).astype(BF16)

    def proj(col):
        return _dot(hn_s[...], win_ref[:, col:col + COL_BLOCK])

    def proj_q(c):
        qf = _silu(proj(OFF_QH + c * COL_BLOCK))
        qh_s[2 * c] = qf[:, :LANES]
        qh_s[2 * c + 1] = qf[:, LANES:]

    def proj_f(c):
        cols = slice(c * COL_BLOCK, (c + 1) * COL_BLOCK)
        sig = jax.nn.sigmoid(proj(OFF_F + c * COL_BLOCK))
        oml = oml_ref[:, cols]
        lf = jnp.log2(lbf_ref[:, cols] + oml * sig)
        k = oml * (1.0 - sig) - lbd_ref[:, cols]
        lf_s[2 * c] = lf[:, :LANES]
        lf_s[2 * c + 1] = lf[:, LANES:]
        kh_s[2 * c] = k[:, :LANES]
        kh_s[2 * c + 1] = k[:, LANES:]

    def proj_i(c):
        res = proj(OFF_I + c * COL_BLOCK).astype(BF16)
        vh_s[2 * c] = res[:, :LANES]
        vh_s[2 * c + 1] = res[:, LANES:]

    def proj_z(c):
        res = proj(OFF_ZH + c * COL_BLOCK)
        zh_s[2 * c] = res[:, :LANES]
        zh_s[2 * c + 1] = res[:, LANES:]

    def proj_qa(c):
        res = proj(OFF_QA + c * COL_BLOCK)
        qa_s[2 * c] = _rope(res[:, :LANES], cosq_ref[...], sinq_ref[...], lo_half).astype(BF16)
        qa_s[2 * c + 1] = _rope(res[:, LANES:], cosq_ref[...], sinq_ref[...], lo_half).astype(BF16)

    def proj_kv():
        res = proj(OFF_KV)
        kr = _rope(res[:, :LANES], cosk_ref[...], sink_rot_ref[...], lo_half)
        kr_sw = pltpu.roll(kr, ATT_DIM, 1)
        va = res[:, LANES:]
        va_sw = pltpu.roll(va, ATT_DIM, 1)
        cur = slice(CHUNK, CHUNK + T)
        zero = jnp.zeros((T, LANES), F32)
        kpad_s[0, cur, :] = jnp.where(lo_head, kr, zero).astype(BF16)
        kpad_s[1, cur, :] = jnp.where(lo_head, zero, kr_sw).astype(BF16)
        kpad_s[2, cur, :] = jnp.where(lo_head, kr_sw, zero).astype(BF16)
        kpad_s[3, cur, :] = jnp.where(lo_head, zero, kr).astype(BF16)
        vpad_s[0, cur, :] = jnp.where(lo_head, va, zero).astype(BF16)
        vpad_s[1, cur, :] = jnp.where(lo_head, zero, va_sw).astype(BF16)
        vpad_s[2, cur, :] = jnp.where(lo_head, va_sw, zero).astype(BF16)
        vpad_s[3, cur, :] = jnp.where(lo_head, zero, va).astype(BF16)

    def proj_za(c):
        res = proj(OFF_ZA + c * COL_BLOCK)
        za_s[2 * c] = res[:, :LANES]
        za_s[2 * c + 1] = res[:, LANES:]

    def cumsum_pair(c):
        for ch in range(T // CHUNK):
            rows = slice(ch * CHUNK, (ch + 1) * CHUNK)
            lf = jnp.concatenate([lf_s[2 * c, rows, :], lf_s[2 * c + 1, rows, :]], axis=1)
            hi = lf.astype(BF16)
            lo = (lf - hi.astype(F32)).astype(BF16)
            b = _dot(tri_ref[...], jnp.concatenate([hi, lo], axis=0))
            b_s[2 * c, rows, :] = b[:, :LANES]
            b_s[2 * c + 1, rows, :] = b[:, LANES:]
            for q0 in range(0, CHUNK, QUARTER):
                first = max(q0 - 1, 0)
                worst_spans.append(b[first:first + 1, :] - b[q0 + QUARTER - 1:q0 + QUARTER, :])

    def hgrn_load(h, ch):
        rows = pl.ds(ch * CHUNK, CHUNK)
        return rows, qh_s[h, rows, :], kh_s[h, rows, :], vh_s[h, rows, :], b_s[h, rows, :], state_s[h]

    def hgrn_finish(h, rows, q, k, v, b, st, scores, diag):
        b_last = b[CHUNK - 1:CHUNK, :]
        q_in = (q * jnp.exp2(b)).astype(BF16)
        o = _dot(scores.astype(BF16), v) + _dot_nt(q_in, st.astype(BF16))
        if diag is not None:
            o = o + diag * v.astype(F32)
        k_out = (k * jnp.exp2(b_last - b)).astype(BF16)
        state_s[h] = st * jnp.exp2(b_last) + _dot_tn(v, k_out)
        ms = jnp.mean(o * o, axis=-1, keepdims=True)
        on = o * lax.rsqrt(ms + NORM_EPS) * ghead
        yin_s[h, rows, :] = (on * _silu(zh_s[h, rows, :])).astype(BF16)

    def hgrn_head_robust(h, ch):
        rows, q, k, v, b, st = hgrn_load(h, ch)
        lf = lf_s[h, rows, :]
        scores = jnp.zeros((CHUNK, CHUNK), F32)
        for level in range(VREG_LEVEL):
            xk = _level_operands(q, k, b, lf, level, row_c)[0].astype(BF16)
            scores = jnp.where(lvl == level, _dot_nt(xk, xk), scores)
        prods = {}
        for level in range(VREG_LEVEL, LEVELS):
            xk, xr = _level_operands(q, k, b, lf, level, row_c)
            prods[level] = _dot_nt(xr.astype(BF16), xk.astype(BF16))
        groups = []
        for g in range(CHUNK // SUBLANES):
            rs = slice(g * SUBLANES, (g + 1) * SUBLANES)
            acc = scores[rs]
            for level in range(VREG_LEVEL, LEVELS):
                if (g * SUBLANES) & (1 << level):
                    ri = _right_group_index(g, level)
                    acc = jnp.where(lvl[rs] == level, prods[level][ri * SUBLANES:(ri + 1) * SUBLANES], acc)
            groups.append(acc)
        scores = jnp.concatenate(groups, axis=0)
        hgrn_finish(h, rows, q, k, v, b, st, scores, jnp.sum(q * k, axis=-1, keepdims=True))

    def hgrn_fast_scores(h, ch):
        rows, q, k, v, b, st = hgrn_load(h, ch)
        mid_lo = jnp.broadcast_to(b[QUARTER - 1:QUARTER, :], (HALF, LANES))
        mid_hi = jnp.broadcast_to(b[HALF + QUARTER - 1:HALF + QUARTER, :], (HALF, LANES))
        d = b - jnp.concatenate([mid_lo, mid_hi], axis=0)
        q_own = (q * jnp.exp2(d)).astype(BF16)
        k_own = (k * jnp.exp2(-d)).astype(BF16)
        q_cross = (q[HALF:] * jnp.exp2(b[HALF:] - mid_lo)).astype(BF16)
        res = _dot_nt(jnp.concatenate([q_own, q_cross], axis=0), k_own)
        b_last = b[CHUNK - 1:CHUNK, :]
        return dict(rows=rows, res=res, v=v, st=st, q_in=(q * jnp.exp2(b)).astype(BF16),
                    k_out=(k * jnp.exp2(b_last - b)).astype(BF16), keep=jnp.exp2(b_last))

    def hgrn_fast_mix(h, t):
        res, v, st = t["res"], t["v"], t["st"]
        first_half_keys = lax.broadcasted_iota(jnp.int32, (HALF, LANES), 1) < HALF
        bottom = jnp.where(first_half_keys, res[CHUNK:], res[HALF:CHUNK])
        scores = jnp.where(causal, jnp.concatenate([res[:HALF], bottom], axis=0), 0.0)
        o = _dot(scores.astype(BF16), v) + _dot_nt(t["q_in"], st.astype(BF16))
        state_s[h] = st * t["keep"] + _dot_tn(v, t["k_out"])
        return o

    def hgrn_fast_gate(h, t, o):
        ms = jnp.mean(o * o, axis=-1, keepdims=True)
        on = o * lax.rsqrt(ms + NORM_EPS) * ghead
        yin_s[h, t["rows"], :] = (on * _silu(zh_s[h, t["rows"], :])).astype(BF16)

    def hgrn_heads_fast(heads, ch):
        parts = [hgrn_fast_scores(h, ch) for h in heads]
        outs = [hgrn_fast_mix(h, t) for h, t in zip(heads, parts)]
        for h, t, o in zip(heads, parts, outs):
            hgrn_fast_gate(h, t, o)

    def attn_scores(n, g, pos):
        qrows = slice(n * CHUNK, (n + 1) * CHUNK)
        krows = slice(n * CHUNK, (n + 2) * CHUNK)
        pairs = range(g * PAIRS_PER_KV, (g + 1) * PAIRS_PER_KV)
        q_stack = jnp.concatenate([qa_s[p, qrows, :] for p in pairs], axis=0)
        s_all = _dot_nt(q_stack, kpad_s[2 * g + pos, krows, :])
        for i, p in enumerate(pairs):
            head = 2 * p + pos
            s = s_all[i * CHUNK:(i + 1) * CHUNK]
            s_prev = jnp.where(jnp.logical_or(j > 0, n > 0), s[:, :CHUNK], NEG_INF)
            sc = jnp.where(causal, s[:, CHUNK:], s_prev)
            sc_s[n % 2 * ATT_HEADS + head] = sc
            m = jnp.maximum(jnp.max(sc, axis=-1, keepdims=True), sink_ref[head])
            m_s[n % 2 * ATT_HEADS + head] = jnp.broadcast_to(m, (CHUNK, LANES))

    def attn_out(n, g):
        qrows = slice(n * CHUNK, (n + 1) * CHUNK)
        krows = slice(n * CHUNK, (n + 2) * CHUNK)
        pairs = range(g * PAIRS_PER_KV, (g + 1) * PAIRS_PER_KV)
        res = []
        for pos in range(2):
            for i, p in enumerate(pairs):
                head = 2 * p + pos
                pr = jnp.exp(sc_s[n % 2 * ATT_HEADS + head] - m_s[n % 2 * ATT_HEADS + head])
                pcat = jnp.concatenate([jnp.where(causal, 0.0, pr), jnp.where(causal, pr, 0.0)], axis=1)
                pst_s[pos, i * CHUNK:(i + 1) * CHUNK, :] = pcat.astype(BF16)
            vp = jnp.concatenate([vpad_s[2 * g + pos, krows, :], ones_blk], axis=1)
            res.append(_dot(pst_s[pos], vp))
        for i, p in enumerate(pairs):
            out = None
            for pos in range(2):
                head = 2 * p + pos
                r = res[pos][i * CHUNK:(i + 1) * CHUNK]
                denom = r[:, LANES:] + jnp.exp(sink_ref[head] - m_s[n % 2 * ATT_HEADS + head])
                o = r[:, :LANES] / denom
                out = o if out is None else out + o
            yin_s[HG_HEADS + p, qrows, :] = (out * _silu(za_s[p, qrows, :])).astype(BF16)

    def out_proj_hgrn(nb):
        cols = slice(nb * COL_BLOCK, (nb + 1) * COL_BLOCK)
        yin = jnp.concatenate([yin_s[i] for i in range(HG_HEADS)], axis=1)
        y_s[:, cols] = _dot(yin, wout_ref[0:HG_WIDTH, cols])

    def out_proj_attn(nb):
        cols = slice(nb * COL_BLOCK, (nb + 1) * COL_BLOCK)
        yin = jnp.concatenate([yin_s[HG_HEADS + i] for i in range(ATT_PAIRS)], axis=1)
        y_s[:, cols] = y_s[:, cols] + _dot(yin, wout_ref[HG_WIDTH:MIX_WIDTH, cols])

    def post_norm():
        for rb in range(T // ROW_BLOCK):
            rows = slice(rb * ROW_BLOCK, (rb + 1) * ROW_BLOCK)
            y = y_s[rows, :]
            ms = jnp.mean(y * y, axis=-1, keepdims=True)
            o_ref[0, rows, :] = x_ref[0, rows, :] + y * lax.rsqrt(ms + NORM_EPS) * gpost_ref[...]

    n_chunks = T // CHUNK
    n_pairs = HG_HEADS // 2
    assert ATT_WIDTH // COL_BLOCK == 2 * KV_HEADS and (n_chunks * KV_HEADS) % (D_MODEL // COL_BLOCK) == 0
    pre_norm()
    stbk_s[...] = state_s[...]
    proj_items = []
    for c in range(n_pairs):
        proj_items += [lambda c=c: proj_q(c), lambda c=c: proj_f(c), lambda c=c: proj_i(c),
                       lambda c=c: (proj_z(c), cumsum_pair(c))]
    proj_items += [proj_kv] + [lambda a=a: proj_qa(a) for a in range(ATT_WIDTH // COL_BLOCK)]
    mixer_at = {}
    for grp in range(HG_HEADS // HEAD_GROUP):
        heads = tuple(range(grp * HEAD_GROUP, (grp + 1) * HEAD_GROUP))
        ready = BLOCKS_PER_PAIR * (HEAD_GROUP // 2) * (grp + 1) - 1
        for ch in range(n_chunks):
            mixer_at[ready + GROUP_STAGGER * grp + MIX_SPACING * (ch + 1)] = (heads, ch)
    assert max(mixer_at) < len(proj_items)
    for i, item in enumerate(proj_items):
        item()
        if i in mixer_at:
            hgrn_heads_fast(*mixer_at[i])
    attn_units = [(g, pos) for g in range(KV_HEADS) for pos in range(2)]
    for a in range(ATT_WIDTH // COL_BLOCK):
        proj_za(a)
        attn_scores(0, *attn_units[a])
    out_every = n_chunks * KV_HEADS // (D_MODEL // COL_BLOCK)
    for n in range(n_chunks):
        for g in range(KV_HEADS):
            attn_out(n, g)
            if n + 1 < n_chunks:
                attn_scores(n + 1, g, 0)
                attn_scores(n + 1, g, 1)
            item = n * KV_HEADS + g
            if item % out_every == out_every - 1:
                out_proj_hgrn(item // out_every)

    worst = worst_spans[0]
    for span in worst_spans[1:]:
        worst = jnp.maximum(worst, span)
    fast_ok = jnp.max(worst) <= FAST_SPAN_LIMIT

    @pl.when(jnp.logical_not(fast_ok))
    def _():
        state_s[...] = stbk_s[...]
        for ch in range(n_chunks):
            def head_body(h, carry, ch=ch):
                hgrn_head_robust(h, ch)
                return carry
            lax.fori_loop(0, HG_HEADS, head_body, 0, unroll=ROBUST_HEAD_UNROLL)
        for nb in range(D_MODEL // COL_BLOCK):
            out_proj_hgrn(nb)

    kpad_s[:, 0:CHUNK, :] = kpad_s[:, T:T + CHUNK, :]
    vpad_s[:, 0:CHUNK, :] = vpad_s[:, T:T + CHUNK, :]

    for nb in range(D_MODEL // COL_BLOCK):
        out_proj_attn(nb)
    post_norm()


def _level_matrix():
    t = np.arange(CHUNK)[:, None]
    s = np.arange(CHUNK)[None, :]
    x = np.bitwise_xor(t, s)
    msb = np.where(x > 0, np.floor(np.log2(np.maximum(x, 1))), -1).astype(np.int32)
    return np.where(s < t, msb, -1).astype(np.int32)


def _rope_tables(seq):
    half = ATT_DIM // 2
    inv_freq = ROPE_THETA ** (-jnp.arange(half, dtype=F32) / half)
    ang = jnp.arange(seq, dtype=F32)[:, None] * inv_freq[None, :]
    cos = jnp.tile(jnp.cos(ang), (1, LANES // half))
    sin = jnp.sin(ang)
    sin = jnp.tile(jnp.concatenate([-sin, sin], axis=1), (1, LANES // ATT_DIM))
    return cos, sin


def _layer(x, layer, w_in, w_out, g_pre, g_post, lb, g_head, sinks, tables):
    B, S, _ = x.shape
    T = SEQ_TILE
    cosq, sinq, cosk, sink_rot, lvl, tri = tables
    lbf = jnp.maximum(lb, LB_FLOOR)

    def const(shape):
        return pl.BlockSpec(shape, lambda b, j: (0,) * len(shape), pipeline_mode=pl.Buffered(1))

    def layer_slab(shape):
        return pl.BlockSpec((None,) + shape, lambda b, j: (layer,) + (0,) * len(shape), pipeline_mode=pl.Buffered(1))

    def seq_table():
        return pl.BlockSpec((T, LANES), lambda b, j: (j, 0))

    in_specs = [
        pl.BlockSpec((1, T, D_MODEL), lambda b, j: (b, j, 0)),
        layer_slab((D_MODEL, IN_WIDTH)),
        layer_slab((MIX_WIDTH, D_MODEL)),
        const((1, D_MODEL)),
        const((1, D_MODEL)),
        const((1, HG_WIDTH)),
        const((1, HG_WIDTH)),
        const((1, HG_WIDTH)),
        const((1, HG_DIM)),
        pl.BlockSpec(memory_space=pltpu.SMEM),
        seq_table(), seq_table(), seq_table(), seq_table(),
        const((CHUNK, CHUNK)),
        const((CHUNK, 2 * CHUNK)),
    ]
    scratch = [
        pltpu.VMEM((T, D_MODEL), BF16),
        pltpu.VMEM((HG_HEADS, T, LANES), F32),
        pltpu.VMEM((HG_HEADS, T, LANES), F32),
        pltpu.VMEM((HG_HEADS, T, LANES), BF16),
        pltpu.VMEM((HG_HEADS, T, LANES), F32),
        pltpu.VMEM((HG_HEADS, T, LANES), F32),
        pltpu.VMEM((HG_HEADS, T, LANES), F32),
        pltpu.VMEM((ATT_PAIRS, T, LANES), BF16),
        pltpu.VMEM((2 * KV_HEADS, CHUNK + T, LANES), BF16),
        pltpu.VMEM((2 * KV_HEADS, CHUNK + T, LANES), BF16),
        pltpu.VMEM((ATT_PAIRS, T, LANES), F32),
        pltpu.VMEM((MIX_WIDTH // LANES, T, LANES), BF16),
        pltpu.VMEM((HG_HEADS, HG_DIM, HG_DIM), F32),
        pltpu.VMEM((2 * ATT_HEADS, CHUNK, LANES), F32),
        pltpu.VMEM((2 * ATT_HEADS, CHUNK, LANES), F32),
        pltpu.VMEM((HG_HEADS, HG_DIM, HG_DIM), F32),
        pltpu.VMEM((T, D_MODEL), F32),
        pltpu.VMEM((2, PAIRS_PER_KV * CHUNK, 2 * CHUNK), BF16),
    ]
    return pl.pallas_call(
        _layer_kernel,
        out_shape=jax.ShapeDtypeStruct(x.shape, x.dtype),
        grid=(B, S // T),
        in_specs=in_specs,
        out_specs=pl.BlockSpec((1, T, D_MODEL), lambda b, j: (b, j, 0)),
        scratch_shapes=scratch,
        compiler_params=pltpu.CompilerParams(
            dimension_semantics=("arbitrary", "arbitrary"),
            vmem_limit_bytes=VMEM_LIMIT_BYTES),
        name="hybrid_layer",
    )(x, w_in, w_out, g_pre[None, :], g_post[None, :],
      lbf[None, :], (1.0 - lb)[None, :], (lbf - lb)[None, :], g_head[None, :], sinks,
      cosq, sinq, cosk, sink_rot, lvl, tri)


def kernel(x, w_in, w_out, g_pre, g_post, lb_param, g_head, sinks):
    depth = w_in.shape[0]
    seq = x.shape[1]
    p = jax.nn.softmax(lb_param.astype(F32), axis=0)
    lower_bounds = jnp.cumsum(p, axis=0) - p[0:1]
    cos, sin = _rope_tables(seq)
    tri = jnp.asarray(np.tile(np.tril(np.ones((CHUNK, CHUNK), np.float32)), (1, 2)), BF16)
    tables = (cos * ATT_SCALE, sin * ATT_SCALE, cos, sin, jnp.asarray(_level_matrix()), tri)
    w_in, w_out = w_in.astype(BF16), w_out.astype(BF16)
    for l in range(depth):
        x = _layer(x, l, w_in, w_out, g_pre[l], g_post[l], lower_bounds[l], g_head[l], sinks[l], tables)
    return x
```

```python
import math

import numpy as np
import jax
import jax.numpy as jnp
from jax import lax
from jax.experimental import pallas as pl
from jax.experimental.pallas import tpu as pltpu

D_MODEL = 1024
HG_WIDTH = 1024
HG_HEADS = 8
HG_DIM = 128
ATT_WIDTH = 1024
ATT_HEADS = 16
ATT_DIM = 64
ATT_PAIRS = ATT_HEADS // 2
KV_HEADS = 2
PAIRS_PER_KV = ATT_PAIRS // KV_HEADS
IN_WIDTH = 6400
MIX_WIDTH = 2048
ATT_SCALE = 1.0 / math.sqrt(ATT_DIM)
ROPE_THETA = 10000.0
NORM_EPS = 1e-6
NEG_INF = -1e30
LB_FLOOR = 1e-20

LANES = 128
SUBLANES = 8
CHUNK = 128
HALF = CHUNK // 2
QUARTER = CHUNK // 4
FAST_SPAN_LIMIT = 64.0
LEVELS = 7
VREG_LEVEL = 3
SEQ_TILE = 512
COL_BLOCK = 256
ROW_BLOCK = 64
ROBUST_HEAD_UNROLL = 4
HEAD_GROUP = 4
MIX_SPACING = 1
BLOCKS_PER_PAIR = 4
VMEM_LIMIT_BYTES = 56 * 1024 * 1024

OFF_QH, OFF_F, OFF_I, OFF_ZH, OFF_QA, OFF_KV, OFF_ZA = 0, 1024, 2048, 3072, 4096, 5120, 5376

F32 = jnp.float32
BF16 = jnp.bfloat16


def _dot(a, b):
    return jnp.dot(a, b, preferred_element_type=F32)


def _dot_nt(a, b):
    return lax.dot_general(a, b, (((1,), (1,)), ((), ())), preferred_element_type=F32)


def _dot_tn(a, b):
    return lax.dot_general(a, b, (((0,), (0,)), ((), ())), preferred_element_type=F32)


def _silu(x):
    return x * jax.nn.sigmoid(x)


def _rope(x, cos, sin, lo_half):
    swapped = jnp.where(lo_half, pltpu.roll(x, LANES - ATT_DIM // 2, 1), pltpu.roll(x, ATT_DIM // 2, 1))
    return x * cos + swapped * sin


def _hold_rows(b, period, offset):
    pieces = []
    for blk in range(CHUNK // period):
        row = blk * period + offset
        pieces.append(jnp.broadcast_to(b[row:row + 1, :], (period, LANES)))
    return pieces[0] if len(pieces) == 1 else jnp.concatenate(pieces, axis=0)


def _level_operands(q, k, b, lf, level, row_idx):
    half = 1 << level
    if level >= VREG_LEVEL:
        left, right = [], []
        for blk in range(CHUNK // (2 * half)):
            lo, mid, hi = blk * 2 * half, blk * 2 * half + half, (blk + 1) * 2 * half
            r = jnp.broadcast_to(b[mid - 1:mid, :], (half, LANES))
            left.append(k[lo:mid] * jnp.exp2(r - b[lo:mid]))
            right.append(q[mid:hi] * jnp.exp2(b[mid:hi] - r))
        both = [piece for pair in zip(left, right) for piece in pair]
        return jnp.concatenate(both, axis=0), (right[0] if len(right) == 1 else jnp.concatenate(right, axis=0))
    if 2 * half == SUBLANES:
        arg = -jnp.abs(b - _hold_rows(b, SUBLANES, half - 1))
    elif 4 * half == SUBLANES:
        upper = (row_idx & (SUBLANES - 1)) >= 2 * half
        arg = -jnp.abs(b - jnp.where(upper, _hold_rows(b, SUBLANES, 3 * half - 1), _hold_rows(b, SUBLANES, half - 1)))
    else:
        arg = jnp.where((row_idx & 1) == 1, lf, 0.0)
    is_right = (row_idx & (2 * half - 1)) >= half
    return jnp.where(is_right, q, k) * jnp.exp2(arg), None


def _right_group_index(group, level):
    half_groups = (1 << level) // SUBLANES
    return (group // (2 * half_groups)) * half_groups + group % half_groups


def _layer_kernel(x_ref, win_ref, wout_ref, gpre_ref, gpost_ref, lbf_ref, oml_ref, lbd_ref, ghead_ref,
                  sink_ref, cosq_ref, sinq_ref, cosk_ref, sink_rot_ref, lvl_ref, tri_ref,
                  o_ref,
                  hn_s, qh_s, kh_s, vh_s, lf_s, b_s, zh_s, qa_s, kpad_s, vpad_s, za_s, yin_s, state_s,
                  sc_s, m_s, stbk_s, y_s, pst_s):
    T = SEQ_TILE
    j = pl.program_id(1)

    @pl.when(j == 0)
    def _():
        state_s[...] = jnp.zeros_like(state_s)
        kpad_s[:, 0:CHUNK, :] = jnp.zeros((2 * KV_HEADS, CHUNK, LANES), BF16)
        vpad_s[:, 0:CHUNK, :] = jnp.zeros((2 * KV_HEADS, CHUNK, LANES), BF16)

    lane = lax.broadcasted_iota(jnp.int32, (T, LANES), 1)
    lo_half = (lane & (ATT_DIM - 1)) < ATT_DIM // 2
    lo_head = lane < ATT_DIM
    row_c = lax.broadcasted_iota(jnp.int32, (CHUNK, LANES), 0)
    col_c = lax.broadcasted_iota(jnp.int32, (CHUNK, LANES), 1)
    causal = row_c >= col_c
    lvl = lvl_ref[...]
    ghead = ghead_ref[...]
    ones_blk = jnp.ones((2 * CHUNK, LANES), BF16)
    worst_spans = []

    def pre_norm():
        for rb in range(T // ROW_BLOCK):
            rows = slice(rb * ROW_BLOCK, (rb + 1) * ROW_BLOCK)
            x = x_ref[0, rows, :]
            ms = jnp.mean(x * x, axis=-1, keepdims=True)
            hn_s[rows, :] = (x * lax.rsqrt(ms + NORM_EPS) * gpre_ref[...]).astype(BF16)

    def proj(col):
        return _dot(hn_s[...], win_ref[:, col:col + COL_BLOCK])

    def proj_q(c):
        qf = _silu(proj(OFF_QH + c * COL_BLOCK))
        qh_s[2 * c] = qf[:, :LANES]
        qh_s[2 * c + 1] = qf[:, LANES:]

    def proj_f(c):
        cols = slice(c * COL_BLOCK, (c + 1) * COL_BLOCK)
        sig = jax.nn.sigmoid(proj(OFF_F + c * COL_BLOCK))
        oml = oml_ref[:, cols]
        lf = jnp.log2(lbf_ref[:, cols] + oml * sig)
        k = oml * (1.0 - sig) - lbd_ref[:, cols]
        lf_s[2 * c] = lf[:, :LANES]
        lf_s[2 * c + 1] = lf[:, LANES:]
        kh_s[2 * c] = k[:, :LANES]
        kh_s[2 * c + 1] = k[:, LANES:]

    def proj_i(c):
        res = proj(OFF_I + c * COL_BLOCK).astype(BF16)
        vh_s[2 * c] = res[:, :LANES]
        vh_s[2 * c + 1] = res[:, LANES:]

    def proj_z(c):
        res = proj(OFF_ZH + c * COL_BLOCK)
        zh_s[2 * c] = res[:, :LANES]
        zh_s[2 * c + 1] = res[:, LANES:]

    def proj_qa(c):
        res = proj(OFF_QA + c * COL_BLOCK)
        qa_s[2 * c] = _rope(res[:, :LANES], cosq_ref[...], sinq_ref[...], lo_half).astype(BF16)
        qa_s[2 * c + 1] = _rope(res[:, LANES:], cosq_ref[...], sinq_ref[...], lo_half).astype(BF16)

    def proj_kv():
        res = proj(OFF_KV)
        kr = _rope(res[:, :LANES], cosk_ref[...], sink_rot_ref[...], lo_half)
        kr_sw = pltpu.roll(kr, ATT_DIM, 1)
        va = res[:, LANES:]
        va_sw = pltpu.roll(va, ATT_DIM, 1)
        cur = slice(CHUNK, CHUNK + T)
        zero = jnp.zeros((T, LANES), F32)
        kpad_s[0, cur, :] = jnp.where(lo_head, kr, zero).astype(BF16)
        kpad_s[1, cur, :] = jnp.where(lo_head, zero, kr_sw).astype(BF16)
        kpad_s[2, cur, :] = jnp.where(lo_head, kr_sw, zero).astype(BF16)
        kpad_s[3, cur, :] = jnp.where(lo_head, zero, kr).astype(BF16)
        vpad_s[0, cur, :] = jnp.where(lo_head, va, zero).astype(BF16)
        vpad_s[1, cur, :] = jnp.where(lo_head, zero, va_sw).astype(BF16)
        vpad_s[2, cur, :] = jnp.where(lo_head, va_sw, zero).astype(BF16)
        vpad_s[3, cur, :] = jnp.where(lo_head, zero, va).astype(BF16)

    def proj_za(c):
        res = proj(OFF_ZA + c * COL_BLOCK)
        za_s[2 * c] = res[:, :LANES]
        za_s[2 * c + 1] = res[:, LANES:]

    def cumsum_pair(c):
        for ch in range(T // CHUNK):
            rows = slice(ch * CHUNK, (ch + 1) * CHUNK)
            lf = jnp.concatenate([lf_s[2 * c, rows, :], lf_s[2 * c + 1, rows, :]], axis=1)
            hi = lf.astype(BF16)
            lo = (lf - hi.astype(F32)).astype(BF16)
            b = _dot(tri_ref[...], jnp.concatenate([hi, lo], axis=0))
            b_s[2 * c, rows, :] = b[:, :LANES]
            b_s[2 * c + 1, rows, :] = b[:, LANES:]
            for q0 in range(0, CHUNK, QUARTER):
                first = max(q0 - 1, 0)
                worst_spans.append(b[first:first + 1, :] - b[q0 + QUARTER - 1:q0 + QUARTER, :])

    def hgrn_load(h, ch):
        rows = pl.ds(ch * CHUNK, CHUNK)
        return rows, qh_s[h, rows, :], kh_s[h, rows, :], vh_s[h, rows, :], b_s[h, rows, :], state_s[h]

    def hgrn_finish(h, rows, q, k, v, b, st, scores, diag):
        b_last = b[CHUNK - 1:CHUNK, :]
        q_in = (q * jnp.exp2(b)).astype(BF16)
        o = _dot(scores.astype(BF16), v) + _dot_nt(q_in, st.astype(BF16))
        if diag is not None:
            o = o + diag * v.astype(F32)
        k_out = (k * jnp.exp2(b_last - b)).astype(BF16)
        state_s[h] = st * jnp.exp2(b_last) + _dot_tn(v, k_out)
        ms = jnp.mean(o * o, axis=-1, keepdims=True)
        on = o * lax.rsqrt(ms + NORM_EPS) * ghead
        yin_s[h, rows, :] = (on * _silu(zh_s[h, rows, :])).astype(BF16)

    def hgrn_head_robust(h, ch):
        rows, q, k, v, b, st = hgrn_load(h, ch)
        lf = lf_s[h, rows, :]
        scores = jnp.zeros((CHUNK, CHUNK), F32)
        for level in range(VREG_LEVEL):
            xk = _level_operands(q, k, b, lf, level, row_c)[0].astype(BF16)
            scores = jnp.where(lvl == level, _dot_nt(xk, xk), scores)
        prods = {}
        for level in range(VREG_LEVEL, LEVELS):
            xk, xr = _level_operands(q, k, b, lf, level, row_c)
            prods[level] = _dot_nt(xr.astype(BF16), xk.astype(BF16))
        groups = []
        for g in range(CHUNK // SUBLANES):
            rs = slice(g * SUBLANES, (g + 1) * SUBLANES)
            acc = scores[rs]
            for level in range(VREG_LEVEL, LEVELS):
                if (g * SUBLANES) & (1 << level):
                    ri = _right_group_index(g, level)
                    acc = jnp.where(lvl[rs] == level, prods[level][ri * SUBLANES:(ri + 1) * SUBLANES], acc)
            groups.append(acc)
        scores = jnp.concatenate(groups, axis=0)
        hgrn_finish(h, rows, q, k, v, b, st, scores, jnp.sum(q * k, axis=-1, keepdims=True))

    def hgrn_fast_scores(h, ch):
        rows, q, k, v, b, st = hgrn_load(h, ch)
        mid_lo = jnp.broadcast_to(b[QUARTER - 1:QUARTER, :], (HALF, LANES))
        mid_hi = jnp.broadcast_to(b[HALF + QUARTER - 1:HALF + QUARTER, :], (HALF, LANES))
        d = b - jnp.concatenate([mid_lo, mid_hi], axis=0)
        q_own = (q * jnp.exp2(d)).astype(BF16)
        k_own = (k * jnp.exp2(-d)).astype(BF16)
        q_cross = (q[HALF:] * jnp.exp2(b[HALF:] - mid_lo)).astype(BF16)
        res = _dot_nt(jnp.concatenate([q_own, q_cross], axis=0), k_own)
        b_last = b[CHUNK - 1:CHUNK, :]
        return dict(rows=rows, res=res, v=v, st=st, q_in=(q * jnp.exp2(b)).astype(BF16),
                    k_out=(k * jnp.exp2(b_last - b)).astype(BF16), keep=jnp.exp2(b_last))

    def hgrn_fast_mix(h, t):
        res, v, st = t["res"], t["v"], t["st"]
        first_half_keys = lax.broadcasted_iota(jnp.int32, (HALF, LANES), 1) < HALF
        bottom = jnp.where(first_half_keys, res[CHUNK:], res[HALF:CHUNK])
        scores = jnp.where(causal, jnp.concatenate([res[:HALF], bottom], axis=0), 0.0)
        o = _dot(scores.astype(BF16), v) + _dot_nt(t["q_in"], st.astype(BF16))
        state_s[h] = st * t["keep"] + _dot_tn(v, t["k_out"])
        return o

    def hgrn_fast_gate(h, t, o):
        ms = jnp.mean(o * o, axis=-1, keepdims=True)
        on = o * lax.rsqrt(ms + NORM_EPS) * ghead
        yin_s[h, t["rows"], :] = (on * _silu(zh_s[h, t["rows"], :])).astype(BF16)

    def hgrn_heads_fast(heads, ch):
        parts = [hgrn_fast_scores(h, ch) for h in heads]
        outs = [hgrn_fast_mix(h, t) for h, t in zip(heads, parts)]
        for h, t, o in zip(heads, parts, outs):
            hgrn_fast_gate(h, t, o)

    def attn_scores(n, g, pos):
        qrows = slice(n * CHUNK, (n + 1) * CHUNK)
        krows = slice(n * CHUNK, (n + 2) * CHUNK)
        pairs = range(g * PAIRS_PER_KV, (g + 1) * PAIRS_PER_KV)
        q_stack = jnp.concatenate([qa_s[p, qrows, :] for p in pairs], axis=0)
        s_all = _dot_nt(q_stack, kpad_s[2 * g + pos, krows, :])
        for i, p in enumerate(pairs):
            head = 2 * p + pos
            s = s_all[i * CHUNK:(i + 1) * CHUNK]
            s_prev = jnp.where(jnp.logical_or(j > 0, n > 0), s[:, :CHUNK], NEG_INF)
            sc = jnp.where(causal, s[:, CHUNK:], s_prev)
            sc_s[n % 2 * ATT_HEADS + head] = sc
            m = jnp.maximum(jnp.max(sc, axis=-1, keepdims=True), sink_ref[head])
            m_s[n % 2 * ATT_HEADS + head] = jnp.broadcast_to(m, (CHUNK, LANES))

    def attn_out(n, g):
        qrows = slice(n * CHUNK, (n + 1) * CHUNK)
        krows = slice(n * CHUNK, (n + 2) * CHUNK)
        pairs = range(g * PAIRS_PER_KV, (g + 1) * PAIRS_PER_KV)
        res = []
        for pos in range(2):
            for i, p in enumerate(pairs):
                head = 2 * p + pos
                pr = jnp.exp(sc_s[n % 2 * ATT_HEADS + head] - m_s[n % 2 * ATT_HEADS + head])
                pcat = jnp.concatenate([jnp.where(causal, 0.0, pr), jnp.where(causal, pr, 0.0)], axis=1)
                pst_s[pos, i * CHUNK:(i + 1) * CHUNK, :] = pcat.astype(BF16)
            vp = jnp.concatenate([vpad_s[2 * g + pos, krows, :], ones_blk], axis=1)
            res.append(_dot(pst_s[pos], vp))
        for i, p in enumerate(pairs):
            out = None
            for pos in range(2):
                head = 2 * p + pos
                r = res[pos][i * CHUNK:(i + 1) * CHUNK]
                denom = r[:, LANES:] + jnp.exp(sink_ref[head] - m_s[n % 2 * ATT_HEADS + head])
                o = r[:, :LANES] / denom
                out = o if out is None else out + o
            yin_s[HG_HEADS + p, qrows, :] = (out * _silu(za_s[p, qrows, :])).astype(BF16)

    def out_proj_hgrn(nb):
        cols = slice(nb * COL_BLOCK, (nb + 1) * COL_BLOCK)
        yin = jnp.concatenate([yin_s[i] for i in range(HG_HEADS)], axis=1)
        y_s[:, cols] = _dot(yin, wout_ref[0:HG_WIDTH, cols])

    def out_proj_attn(nb):
        cols = slice(nb * COL_BLOCK, (nb + 1) * COL_BLOCK)
        yin = jnp.concatenate([yin_s[HG_HEADS + i] for i in range(ATT_PAIRS)], axis=1)
        y_s[:, cols] = y_s[:, cols] + _dot(yin, wout_ref[HG_WIDTH:MIX_WIDTH, cols])

    def post_norm():
        for rb in range(T // ROW_BLOCK):
            rows = slice(rb * ROW_BLOCK, (rb + 1) * ROW_BLOCK)
            y = y_s[rows, :]
            ms = jnp.mean(y * y, axis=-1, keepdims=True)
            o_ref[0, rows, :] = x_ref[0, rows, :] + y * lax.rsqrt(ms + NORM_EPS) * gpost_ref[...]

    n_chunks = T // CHUNK
    n_pairs = HG_HEADS // 2
    assert ATT_WIDTH // COL_BLOCK == 2 * KV_HEADS and (n_chunks * KV_HEADS) % (D_MODEL // COL_BLOCK) == 0
    pre_norm()
    stbk_s[...] = state_s[...]
    proj_items = []
    for c in range(n_pairs):
        proj_items += [lambda c=c: proj_q(c), lambda c=c: proj_f(c), lambda c=c: proj_i(c),
                       lambda c=c: (proj_z(c), cumsum_pair(c))]
    proj_items += [proj_kv] + [lambda a=a: proj_qa(a) for a in range(ATT_WIDTH // COL_BLOCK)]
    mixer_at = {}
    for grp in range(HG_HEADS // HEAD_GROUP):
        heads = tuple(range(grp * HEAD_GROUP, (grp + 1) * HEAD_GROUP))
        ready = BLOCKS_PER_PAIR * (HEAD_GROUP // 2) * (grp + 1) - 1
        for ch in range(n_chunks):
            mixer_at[ready + MIX_SPACING * (ch + 1)] = (heads, ch)
    assert max(mixer_at) < len(proj_items)
    for i, item in enumerate(proj_items):
        item()
        if i in mixer_at:
            hgrn_heads_fast(*mixer_at[i])
    attn_units = [(g, pos) for g in range(KV_HEADS) for pos in range(2)]
    for a in range(ATT_WIDTH // COL_BLOCK):
        proj_za(a)
        attn_scores(0, *attn_units[a])
    out_every = n_chunks * KV_HEADS // (D_MODEL // COL_BLOCK)
    for n in range(n_chunks):
        for g in range(KV_HEADS):
            attn_out(n, g)
            if n + 1 < n_chunks:
                attn_scores(n + 1, g, 0)
                attn_scores(n + 1, g, 1)
            item = n * KV_HEADS + g
            if item % out_every == out_every - 1:
                out_proj_hgrn(item // out_every)

    worst = worst_spans[0]
    for span in worst_spans[1:]:
        worst = jnp.maximum(worst, span)
    fast_ok = jnp.max(worst) <= FAST_SPAN_LIMIT

    @pl.when(jnp.logical_not(fast_ok))
    def _():
        state_s[...] = stbk_s[...]
        for ch in range(n_chunks):
            def head_body(h, carry, ch=ch):
                hgrn_head_robust(h, ch)
                return carry
            lax.fori_loop(0, HG_HEADS, head_body, 0, unroll=ROBUST_HEAD_UNROLL)
        for nb in range(D_MODEL // COL_BLOCK):
            out_proj_hgrn(nb)

    kpad_s[:, 0:CHUNK, :] = kpad_s[:, T:T + CHUNK, :]
    vpad_s[:, 0:CHUNK, :] = vpad_s[:, T:T + CHUNK, :]

    for nb in range(D_MODEL // COL_BLOCK):
        out_proj_attn(nb)
    post_norm()


def _level_matrix():
    t = np.arange(CHUNK)[:, None]
    s = np.arange(CHUNK)[None, :]
    x = np.bitwise_xor(t, s)
    msb = np.where(x > 0, np.floor(np.log2(np.maximum(x, 1))), -1).astype(np.int32)
    return np.where(s < t, msb, -1).astype(np.int32)


def _rope_tables(seq):
    half = ATT_DIM // 2
    inv_freq = ROPE_THETA ** (-jnp.arange(half, dtype=F32) / half)
    ang = jnp.arange(seq, dtype=F32)[:, None] * inv_freq[None, :]
    cos = jnp.tile(jnp.cos(ang), (1, LANES // half))
    sin = jnp.sin(ang)
    sin = jnp.tile(jnp.concatenate([-sin, sin], axis=1), (1, LANES // ATT_DIM))
    return cos, sin


def _layer(x, layer, w_in, w_out, g_pre, g_post, lb, g_head, sinks, tables):
    B, S, _ = x.shape
    T = SEQ_TILE
    cosq, sinq, cosk, sink_rot, lvl, tri = tables
    lbf = jnp.maximum(lb, LB_FLOOR)

    def const(shape):
        return pl.BlockSpec(shape, lambda b, j: (0,) * len(shape), pipeline_mode=pl.Buffered(1))

    def layer_slab(shape):
        return pl.BlockSpec((None,) + shape, lambda b, j: (layer,) + (0,) * len(shape), pipeline_mode=pl.Buffered(1))

    def seq_table():
        return pl.BlockSpec((T, LANES), lambda b, j: (j, 0))

    in_specs = [
        pl.BlockSpec((1, T, D_MODEL), lambda b, j: (b, j, 0)),
        layer_slab((D_MODEL, IN_WIDTH)),
        layer_slab((MIX_WIDTH, D_MODEL)),
        const((1, D_MODEL)),
        const((1, D_MODEL)),
        const((1, HG_WIDTH)),
        const((1, HG_WIDTH)),
        const((1, HG_WIDTH)),
        const((1, HG_DIM)),
        pl.BlockSpec(memory_space=pltpu.SMEM),
        seq_table(), seq_table(), seq_table(), seq_table(),
        const((CHUNK, CHUNK)),
        const((CHUNK, 2 * CHUNK)),
    ]
    scratch = [
        pltpu.VMEM((T, D_MODEL), BF16),
        pltpu.VMEM((HG_HEADS, T, LANES), F32),
        pltpu.VMEM((HG_HEADS, T, LANES), F32),
        pltpu.VMEM((HG_HEADS, T, LANES), BF16),
        pltpu.VMEM((HG_HEADS, T, LANES), F32),
        pltpu.VMEM((HG_HEADS, T, LANES), F32),
        pltpu.VMEM((HG_HEADS, T, LANES), F32),
        pltpu.VMEM((ATT_PAIRS, T, LANES), BF16),
        pltpu.VMEM((2 * KV_HEADS, CHUNK + T, LANES), BF16),
        pltpu.VMEM((2 * KV_HEADS, CHUNK + T, LANES), BF16),
        pltpu.VMEM((ATT_PAIRS, T, LANES), F32),
        pltpu.VMEM((MIX_WIDTH // LANES, T, LANES), BF16),
        pltpu.VMEM((HG_HEADS, HG_DIM, HG_DIM), F32),
        pltpu.VMEM((2 * ATT_HEADS, CHUNK, LANES), F32),
        pltpu.VMEM((2 * ATT_HEADS, CHUNK, LANES), F32),
        pltpu.VMEM((HG_HEADS, HG_DIM, HG_DIM), F32),
        pltpu.VMEM((T, D_MODEL), F32),
        pltpu.VMEM((2, PAIRS_PER_KV * CHUNK, 2 * CHUNK), BF16),
    ]
    return pl.pallas_call(
        _layer_kernel,
        out_shape=jax.ShapeDtypeStruct(x.shape, x.dtype),
        grid=(B, S // T),
        in_specs=in_specs,
        out_specs=pl.BlockSpec((1, T, D_MODEL), lambda b, j: (b, j, 0)),
        scratch_shapes=scratch,
        compiler_params=pltpu.CompilerParams(
            dimension_semantics=("arbitrary", "arbitrary"),
            vmem_limit_bytes=VMEM_LIMIT_BYTES),
        name="hybrid_layer",
    )(x, w_in, w_out, g_pre[None, :], g_post[None, :],
      lbf[None, :], (1.0 - lb)[None, :], (lbf - lb)[None, :], g_head[None, :], sinks,
      cosq, sinq, cosk, sink_rot, lvl, tri)


def kernel(x, w_in, w_out, g_pre, g_post, lb_param, g_head, sinks):
    depth = w_in.shape[0]
    seq = x.shape[1]
    p = jax.nn.softmax(lb_param.astype(F32), axis=0)
    lower_bounds = jnp.cumsum(p, axis=0) - p[0:1]
    cos, sin = _rope_tables(seq)
    tri = jnp.asarray(np.tile(np.tril(np.ones((CHUNK, CHUNK), np.float32)), (1, 2)), BF16)
    tables = (cos * ATT_SCALE, sin * ATT_SCALE, cos, sin, jnp.asarray(_level_matrix()), tri)
    w_in, w_out = w_in.astype(BF16), w_out.astype(BF16)
    for l in range(depth):
        x = _layer(x, l, w_in, w_out, g_pre[l], g_post[l], lower_bounds[l], g_head[l], sinks[l], tables)
    return x
```
